```python
import math
import jax
import jax.numpy as jnp
from jax import lax

D_MODEL = 2048
BATCH = 1
SEQ = 16384
DEPTH = 2

RMS_EPS = 1e-6
L2_EPS = 1e-6
D_FF = 5632
CONV_WIDTH = 4

GLA_HEADS = 4
GLA_KEY = D_MODEL // 4
GLA_VAL = D_MODEL // 2
GLA_DK = GLA_KEY // GLA_HEADS
GLA_DV = GLA_VAL // GLA_HEADS
GLA_GATE_RANK = 16
GLA_GATE_NORMALIZER = 16.0
GLA_CHUNK = 64

LRU_WIDTH = D_MODEL // 2
LRU_BLOCKS = 8
LRU_BLOCK = LRU_WIDTH // LRU_BLOCKS
LRU_C = 8.0

A_IN = 2 * GLA_KEY + 2 * GLA_VAL + GLA_GATE_RANK + 2 * LRU_WIDTH
A_MIX = GLA_VAL + LRU_WIDTH

GDN_QK_HEADS = 16
GDN_V_HEADS = 32
GDN_DK = D_MODEL // 16
GDN_DV = D_MODEL // 16
GDN_KEY = GDN_QK_HEADS * GDN_DK
GDN_VAL = GDN_V_HEADS * GDN_DV
GDN_CONV_CH = 2 * GDN_KEY + GDN_VAL
C_IN = GDN_CONV_CH + GDN_VAL + 2 * GDN_V_HEADS
GDN_CHUNK = 64

N_A_LAYERS = (DEPTH + 1) // 2
N_C_LAYERS = DEPTH // 2

kernel_name = "hybrid_gla_rglru_gdn_macaron"


def rms_norm(x, w):
    xf = x.astype(jnp.float32)
    y = xf * lax.rsqrt(jnp.mean(xf * xf, axis=-1, keepdims=True) + RMS_EPS)
    return (y * w.astype(jnp.float32)).astype(x.dtype)


def l2_normalize(t):
    return t * lax.rsqrt(jnp.sum(t * t, axis=-1, keepdims=True) + L2_EPS)


def swiglu(x, w_gate, w_up, w_down):
    return (jax.nn.silu(x @ w_gate) * (x @ w_up)) @ w_down


def causal_depthwise_conv(x, w):
    ch = x.shape[-1]
    return lax.conv_general_dilated(
        x, w[:, None, :].astype(x.dtype), window_strides=(1,),
        padding=[(CONV_WIDTH - 1, 0)], dimension_numbers=('NWC', 'WIO', 'NWC'),
        feature_group_count=ch)


def to_chunks(t, chunk):
    b, s, h = t.shape[:3]
    return t.reshape(b, s // chunk, chunk, h, -1).transpose(1, 0, 3, 2, 4)


def from_chunks(t):
    n, b, h, c, d = t.shape
    return t.transpose(1, 0, 3, 2, 4).reshape(b, n * c, h, d)


def gla_chunked(q, k, v, log_a):
    b, s, h, dk = q.shape
    dv = v.shape[-1]
    causal = jnp.tril(jnp.ones((GLA_CHUNK, GLA_CHUNK), bool))

    def step(state, inp):
        qc, kc, vc, lac = inp
        cum = jnp.cumsum(lac, axis=2)
        mid = cum[:, :, GLA_CHUNK // 2 - 1][:, :, None]
        last = cum[:, :, -1:]
        scores = jnp.einsum('bhid,bhjd->bhij', qc * jnp.exp(cum - mid), kc * jnp.exp(mid - cum))
        scores = jnp.where(causal, scores, 0.0)
        out = (jnp.einsum('bhij,bhjv->bhiv', scores, vc)
               + jnp.einsum('bhid,bhdv->bhiv', qc * jnp.exp(cum), state))
        state = (jnp.exp(last)[:, :, 0, :, None] * state
                 + jnp.einsum('bhjd,bhjv->bhdv', kc * jnp.exp(last - cum), vc))
        return state, out

    s0 = jnp.zeros((b, h, dk, dv), jnp.float32)
    _, out = lax.scan(step, s0, (to_chunks(q, GLA_CHUNK), to_chunks(k, GLA_CHUNK),
                                 to_chunks(v, GLA_CHUNK), to_chunks(log_a, GLA_CHUNK)))
    return from_chunks(out)


def rg_lru(xc, w_a, b_a, w_x, b_x, lam):
    b, s, w = xc.shape
    xb = xc.reshape(b, s, LRU_BLOCKS, LRU_BLOCK)
    f32 = jnp.float32
    r = jax.nn.sigmoid(jnp.einsum('bsni,nij->bsnj', xb, w_a.astype(f32)).reshape(b, s, w) + b_a.astype(f32))
    i = jax.nn.sigmoid(jnp.einsum('bsni,nij->bsnj', xb, w_x.astype(f32)).reshape(b, s, w) + b_x.astype(f32))
    log_a = -LRU_C * r * jax.nn.softplus(-lam.astype(f32))
    a = jnp.exp(log_a)
    u = jnp.sqrt(-jnp.expm1(2.0 * log_a)) * (i * xc)

    def combine(left, right):
        a_l, u_l = left
        a_r, u_r = right
        return a_l * a_r, a_r * u_l + u_r

    _, hseq = lax.associative_scan(combine, (a, u), axis=1)
    return hseq


def gated_delta_chunked(q, k, v, beta, g):
    b, s, h, dk = q.shape
    dv = v.shape[-1]
    c = GDN_CHUNK
    incl = jnp.tril(jnp.ones((c, c), bool))
    strict = jnp.tril(jnp.ones((c, c), bool), -1)
    eye = jnp.eye(c, dtype=jnp.float32)

    def step(state, inp):
        qc, kc, vc, bc, gc = inp
        gcum = jnp.cumsum(gc[..., 0], axis=-1)
        decay = jnp.exp(jnp.where(incl, gcum[..., :, None] - gcum[..., None, :], -jnp.inf))
        kb = kc * bc
        lmat = jnp.where(strict, jnp.einsum('bhid,bhjd->bhij', kb, kc) * decay, 0.0)
        rhs = jnp.concatenate([vc * bc, kb * jnp.exp(gcum)[..., None]], axis=-1)
        sol = lax.linalg.triangular_solve(eye + lmat, rhs, left_side=True, lower=True,
                                          unit_diagonal=True)
        u_c, w_c = sol[..., :dv], sol[..., dv:]
        v_new = u_c - jnp.einsum('bhcd,bhdv->bhcv', w_c, state)
        attn = jnp.einsum('bhid,bhjd->bhij', qc, kc) * decay
        out = (jnp.einsum('bhid,bhdv->bhiv', qc * jnp.exp(gcum)[..., None], state)
               + jnp.einsum('bhij,bhjv->bhiv', attn, v_new))
        glast = gcum[..., -1:]
        state = (jnp.exp(glast)[..., None] * state
                 + jnp.einsum('bhcd,bhcv->bhdv', kc * jnp.exp(glast - gcum)[..., None], v_new))
        return state, out

    s0 = jnp.zeros((b, h, dk, dv), jnp.float32)
    _, out = lax.scan(step, s0, (to_chunks(q, c), to_chunks(k, c), to_chunks(v, c),
                                 to_chunks(beta, c), to_chunks(g, c)))
    return from_chunks(out)


def mixer_gla_rglru(hn, w_in, gla_gate_w2, gla_gate_b, gla_norm, rg_conv_w, rg_conv_b,
                    rg_w_a, rg_b_a, rg_w_x, rg_b_x, rg_lambda, w_out):
    b, s, _ = hn.shape
    f32 = jnp.float32
    proj = hn @ w_in
    cuts = [GLA_KEY, 2 * GLA_KEY, 2 * GLA_KEY + GLA_VAL, 2 * GLA_KEY + 2 * GLA_VAL,
            2 * GLA_KEY + 2 * GLA_VAL + GLA_GATE_RANK,
            2 * GLA_KEY + 2 * GLA_VAL + GLA_GATE_RANK + LRU_WIDTH]
    q, k, v, og, lr, xr, yr = jnp.split(proj, cuts, axis=-1)
    log_a = jax.nn.log_sigmoid((lr @ gla_gate_w2 + gla_gate_b).astype(f32)) / GLA_GATE_NORMALIZER
    qh = q.reshape(b, s, GLA_HEADS, GLA_DK).astype(f32) * (GLA_DK ** -0.5)
    kh = k.reshape(b, s, GLA_HEADS, GLA_DK).astype(f32)
    vh = v.reshape(b, s, GLA_HEADS, GLA_DV).astype(f32)
    o = gla_chunked(qh, kh, vh, log_a.reshape(b, s, GLA_HEADS, GLA_DK))
    o = rms_norm(o, gla_norm) * jax.nn.silu(og.reshape(b, s, GLA_HEADS, GLA_DV).astype(f32))
    o_gla = o.astype(hn.dtype).reshape(b, s, GLA_VAL)
    xc = causal_depthwise_conv(xr, rg_conv_w) + rg_conv_b
    hr = rg_lru(xc.astype(f32), rg_w_a, rg_b_a, rg_w_x, rg_b_x, rg_lambda)
    o_lru = hr.astype(hn.dtype) * jax.nn.gelu(yr)
    return jnp.concatenate([o_gla, o_lru], axis=-1) @ w_out


def mixer_gated_deltanet(hn, w_in, conv_w, a_log, dt_bias, out_norm, w_out):
    b, s, _ = hn.shape
    f32 = jnp.float32
    proj = hn @ w_in
    qkv, z, beta_logit, a = jnp.split(
        proj, [GDN_CONV_CH, GDN_CONV_CH + GDN_VAL, GDN_CONV_CH + GDN_VAL + GDN_V_HEADS], axis=-1)
    qkv = jax.nn.silu(causal_depthwise_conv(qkv, conv_w))
    q, k, v = jnp.split(qkv, [GDN_KEY, 2 * GDN_KEY], axis=-1)
    rep = GDN_V_HEADS // GDN_QK_HEADS
    q = jnp.repeat(l2_normalize(q.reshape(b, s, GDN_QK_HEADS, GDN_DK).astype(f32)), rep, axis=2)
    q = q * (GDN_DK ** -0.5)
    k = jnp.repeat(l2_normalize(k.reshape(b, s, GDN_QK_HEADS, GDN_DK).astype(f32)), rep, axis=2)
    v = v.reshape(b, s, GDN_V_HEADS, GDN_DV).astype(f32)
    beta = jax.nn.sigmoid(beta_logit.astype(f32))
    g = -jnp.exp(a_log.astype(f32)) * jax.nn.softplus(a.astype(f32) + dt_bias.astype(f32))
    o = gated_delta_chunked(q, k, v, beta, g)
    o = rms_norm(o, out_norm) * jax.nn.silu(z.reshape(b, s, GDN_V_HEADS, GDN_DV).astype(f32))
    return o.astype(hn.dtype).reshape(b, s, GDN_VAL) @ w_out


def setup_inputs(seed: int = 0) -> dict:
    key = jax.random.key(seed)
    ks = jax.random.split(key, 24)
    f32 = jnp.float32

    def dense(k, shape, fan_in):
        return jax.random.normal(k, shape, f32) * (fan_in ** -0.5)

    def gain(k, shape):
        return 1.0 + 0.02 * jax.random.normal(k, shape, f32)

    def small(k, shape):
        return 0.02 * jax.random.normal(k, shape, f32)

    x = jax.random.normal(ks[0], (BATCH, SEQ, D_MODEL), f32)
    norms = gain(ks[1], (DEPTH, 3, D_MODEL))
    ffn_w_gate = dense(ks[2], (DEPTH, 2, D_MODEL, D_FF), D_MODEL)
    ffn_w_up = dense(ks[3], (DEPTH, 2, D_MODEL, D_FF), D_MODEL)
    ffn_w_down = dense(ks[4], (DEPTH, 2, D_FF, D_MODEL), D_FF)
    a_w_in = dense(ks[5], (N_A_LAYERS, D_MODEL, A_IN), D_MODEL)
    a_gla_gate_w2 = dense(ks[6], (N_A_LAYERS, GLA_GATE_RANK, GLA_KEY), GLA_GATE_RANK)
    a_gla_gate_b = small(ks[7], (N_A_LAYERS, GLA_KEY))
    a_gla_norm = gain(ks[8], (N_A_LAYERS, GLA_DV))
    a_rg_conv_w = dense(ks[9], (N_A_LAYERS, CONV_WIDTH, LRU_WIDTH), CONV_WIDTH)
    a_rg_conv_b = small(ks[10], (N_A_LAYERS, LRU_WIDTH))
    a_rg_w_a = dense(ks[11], (N_A_LAYERS, LRU_BLOCKS, LRU_BLOCK, LRU_BLOCK), LRU_BLOCK)
    a_rg_b_a = small(ks[12], (N_A_LAYERS, LRU_WIDTH))
    a_rg_w_x = dense(ks[13], (N_A_LAYERS, LRU_BLOCKS, LRU_BLOCK, LRU_BLOCK), LRU_BLOCK)
    a_rg_b_x = small(ks[14], (N_A_LAYERS, LRU_WIDTH))
    a_pow_c = jax.random.uniform(ks[15], (N_A_LAYERS, LRU_WIDTH), f32, 0.9, 0.999)
    p = a_pow_c ** (1.0 / LRU_C)
    a_rg_lambda = jnp.log(p) - jnp.log1p(-p)
    a_w_out = dense(ks[16], (N_A_LAYERS, A_MIX, D_MODEL), A_MIX)
    c_w_in = dense(ks[17], (N_C_LAYERS, D_MODEL, C_IN), D_MODEL)
    c_conv_w = dense(ks[18], (N_C_LAYERS, CONV_WIDTH, GDN_CONV_CH), CONV_WIDTH)
    c_a_log = jnp.log(jax.random.uniform(ks[19], (N_C_LAYERS, GDN_V_HEADS), f32, 1.0, 16.0))
    dt = jnp.exp(jax.random.uniform(ks[20], (N_C_LAYERS, GDN_V_HEADS), f32,
                                    math.log(1e-3), math.log(1e-1)))
    c_dt_bias = dt + jnp.log(-jnp.expm1(-dt))
    c_out_norm = gain(ks[21], (N_C_LAYERS, GDN_DV))
    c_w_out = dense(ks[22], (N_C_LAYERS, GDN_VAL, D_MODEL), GDN_VAL)
    final_norm = gain(ks[23], (D_MODEL,))
    return {
        "x": x, "norms": norms, "ffn_w_gate": ffn_w_gate, "ffn_w_up": ffn_w_up,
        "ffn_w_down": ffn_w_down, "a_w_in": a_w_in, "a_gla_gate_w2": a_gla_gate_w2,
        "a_gla_gate_b": a_gla_gate_b, "a_gla_norm": a_gla_norm, "a_rg_conv_w": a_rg_conv_w,
        "a_rg_conv_b": a_rg_conv_b, "a_rg_w_a": a_rg_w_a, "a_rg_b_a": a_rg_b_a,
        "a_rg_w_x": a_rg_w_x, "a_rg_b_x": a_rg_b_x, "a_rg_lambda": a_rg_lambda,
        "a_w_out": a_w_out, "c_w_in": c_w_in, "c_conv_w": c_conv_w, "c_a_log": c_a_log,
        "c_dt_bias": c_dt_bias, "c_out_norm": c_out_norm, "c_w_out": c_w_out,
        "final_norm": final_norm,
    }


def reference(x, norms, ffn_w_gate, ffn_w_up, ffn_w_down, a_w_in, a_gla_gate_w2, a_gla_gate_b,
              a_gla_norm, a_rg_conv_w, a_rg_conv_b, a_rg_w_a, a_rg_b_a, a_rg_w_x, a_rg_b_x,
              a_rg_lambda, a_w_out, c_w_in, c_conv_w, c_a_log, c_dt_bias, c_out_norm, c_w_out,
              final_norm):
    h = x
    for layer in range(DEPTH):
        j = layer // 2
        h = h + 0.5 * swiglu(rms_norm(h, norms[layer, 0]), ffn_w_gate[layer, 0],
                             ffn_w_up[layer, 0], ffn_w_down[layer, 0])
        hn = rms_norm(h, norms[layer, 1])
        if layer % 2 == 0:
            mix = mixer_gla_rglru(hn, a_w_in[j], a_gla_gate_w2[j], a_gla_gate_b[j], a_gla_norm[j],
                                  a_rg_conv_w[j], a_rg_conv_b[j], a_rg_w_a[j], a_rg_b_a[j],
                                  a_rg_w_x[j], a_rg_b_x[j], a_rg_lambda[j], a_w_out[j])
        else:
            mix = mixer_gated_deltanet(hn, c_w_in[j], c_conv_w[j], c_a_log[j], c_dt_bias[j],
                                       c_out_norm[j], c_w_out[j])
        h = h + mix
        h = h + 0.5 * swiglu(rms_norm(h, norms[layer, 2]), ffn_w_gate[layer, 1],
                             ffn_w_up[layer, 1], ffn_w_down[layer, 1])
    return rms_norm(h, final_norm)
```

```python
import functools

import jax
import jax.numpy as jnp
from jax import lax
from jax.experimental import pallas as pl
from jax.experimental.pallas import tpu as pltpu

F32 = jnp.float32
BF16 = jnp.bfloat16
HIGHEST = lax.Precision.HIGHEST

LANE = 128
D_MODEL = 2048
D_FF = 5632
RMS_EPS = 1e-6
L2_EPS = 1e-6
CONV_WIDTH = 4
CONV_PAD = 8

GLA_HEADS = 4
GLA_DK = 128
GLA_DV = 256
GLA_KEY = GLA_HEADS * GLA_DK
GLA_VAL = GLA_HEADS * GLA_DV
GLA_GATE_RANK = 16
GLA_GATE_NORMALIZER = 16.0
CHUNK = 64

LRU_WIDTH = 1024
LRU_BLOCKS = 8
LRU_C = 8.0

GDN_QK_HEADS = 16
GDN_V_HEADS = 32
GDN_DK = 128
GDN_DV = 128
GDN_KEY = GDN_QK_HEADS * GDN_DK
GDN_VAL = GDN_V_HEADS * GDN_DV

VMEM_LIMIT = 56 * 1024 * 1024


def _params(*sem):
    return pltpu.CompilerParams(dimension_semantics=sem, vmem_limit_bytes=VMEM_LIMIT)


def _rms(x, gain):
    return x * lax.rsqrt(jnp.mean(x * x, axis=-1, keepdims=True) + RMS_EPS) * gain


def _softplus(z):
    return jnp.maximum(z, 0.0) + jnp.log1p(jnp.exp(-jnp.abs(z)))


def _silu(z):
    return z * jax.nn.sigmoid(z)


def _dot(a, b):
    return jnp.dot(a.astype(BF16), b.astype(BF16), preferred_element_type=F32)


def _dot_nt(a, b):
    return lax.dot_general(a.astype(BF16), b.astype(BF16), (((1,), (1,)), ((), ())),
                           preferred_element_type=F32)


def _dot_tn(a, b):
    return lax.dot_general(a.astype(BF16), b.astype(BF16), (((0,), (0,)), ((), ())),
                           preferred_element_type=F32)


def _dot_f32(a, b):
    return jnp.dot(a, b, precision=HIGHEST, preferred_element_type=F32)


def _iota2(shape, dim):
    return lax.broadcasted_iota(jnp.int32, shape, dim)


def _ffn_kernel(x_ref, g_ref, wg_ref, wu_ref, wd_ref, fg_ref, o_ref, n_ref, *, final_norm):
    j = pl.program_id(1)

    @pl.when(j == 0)
    def _():
        x = x_ref[...]
        n_ref[...] = _rms(x, g_ref[...]).astype(BF16)
        o_ref[...] = x

    n = n_ref[...]
    gate = jnp.dot(n, wg_ref[...], preferred_element_type=F32)
    up = jnp.dot(n, wu_ref[...], preferred_element_type=F32)
    act = (0.5 * _silu(gate) * up).astype(BF16)
    o_ref[...] += jnp.dot(act, wd_ref[...], preferred_element_type=F32)

    if final_norm:
        @pl.when(j == pl.num_programs(1) - 1)
        def _():
            o_ref[...] = _rms(o_ref[...], fg_ref[...])


def _ffn(h, gain, wg, wu, wd, final_gain=None, tm=512, tf=512):
    s, d = h.shape
    dff = wg.shape[1]
    fg = jnp.ones((1, d), F32) if final_gain is None else final_gain.reshape(1, d)
    return pl.pallas_call(
        functools.partial(_ffn_kernel, final_norm=final_gain is not None),
        grid=(s // tm, dff // tf),
        in_specs=[
            pl.BlockSpec((tm, d), lambda i, j: (i, 0)),
            pl.BlockSpec((1, d), lambda i, j: (0, 0)),
            pl.BlockSpec((d, tf), lambda i, j: (0, j)),
            pl.BlockSpec((d, tf), lambda i, j: (0, j)),
            pl.BlockSpec((tf, d), lambda i, j: (j, 0)),
            pl.BlockSpec((1, d), lambda i, j: (0, 0)),
        ],
        out_specs=pl.BlockSpec((tm, d), lambda i, j: (i, 0)),
        out_shape=jax.ShapeDtypeStruct((s, d), F32),
        scratch_shapes=[pltpu.VMEM((tm, d), BF16)],
        compiler_params=_params("parallel", "arbitrary"),
        name="ffn",
    )(h, gain.reshape(1, d), wg, wu, wd, fg)


def _norm_proj_kernel(x_ref, g_ref, w_ref, we_ref, o_ref, e_ref, n_ref):
    j = pl.program_id(1)

    @pl.when(j == 0)
    def _():
        nb = _rms(x_ref[...], g_ref[...]).astype(BF16)
        n_ref[...] = nb
        e_ref[...] = jnp.dot(nb, we_ref[...], preferred_element_type=F32)

    acc = jnp.dot(n_ref[...], w_ref[...], preferred_element_type=F32)
    for c in range(o_ref.shape[0]):
        o_ref[c] = acc[:, c * LANE:(c + 1) * LANE]


def _norm_proj(h, gain, w_main, w_extra, tm=512, tn=1024):
    s, d = h.shape
    n = w_main.shape[1]
    return pl.pallas_call(
        _norm_proj_kernel,
        grid=(s // tm, n // tn),
        in_specs=[
            pl.BlockSpec((tm, d), lambda i, j: (i, 0)),
            pl.BlockSpec((1, d), lambda i, j: (0, 0)),
            pl.BlockSpec((d, tn), lambda i, j: (0, j)),
            pl.BlockSpec((d, LANE), lambda i, j: (0, 0)),
        ],
        out_specs=[
            pl.BlockSpec((tn // LANE, tm, LANE), lambda i, j: (j, i, 0)),
            pl.BlockSpec((tm, LANE), lambda i, j: (i, 0)),
        ],
        out_shape=[
            jax.ShapeDtypeStruct((n // LANE, s, LANE), F32),
            jax.ShapeDtypeStruct((s, LANE), F32),
        ],
        scratch_shapes=[pltpu.VMEM((tm, d), BF16)],
        compiler_params=_params("parallel", "arbitrary"),
        name="norm_proj",
    )(h, gain.reshape(1, d), w_main, w_extra)


def _proj_res_kernel(*refs, n_in):
    h_ref = refs[0]
    x_refs = refs[1:1 + n_in]
    w_refs = refs[1 + n_in:1 + 2 * n_in]
    o_ref = refs[1 + 2 * n_in]
    acc = h_ref[...]
    for x_ref, w_ref in zip(x_refs, w_refs):
        acc = acc + jnp.dot(x_ref[...], w_ref[...], preferred_element_type=F32)
    o_ref[...] = acc


def _proj_res(h, xs, ws, tm=512, tn=512):
    s, d = h.shape
    n_in = len(xs)
    in_specs = [pl.BlockSpec((tm, tn), lambda i, j: (i, j))]
    in_specs += [pl.BlockSpec((tm, x.shape[1]), lambda i, j: (i, 0)) for x in xs]
    in_specs += [pl.BlockSpec((w.shape[0], tn), lambda i, j: (0, j)) for w in ws]
    return pl.pallas_call(
        functools.partial(_proj_res_kernel, n_in=n_in),
        grid=(s // tm, d // tn),
        in_specs=in_specs,
        out_specs=pl.BlockSpec((tm, tn), lambda i, j: (i, j)),
        out_shape=jax.ShapeDtypeStruct((s, d), F32),
        compiler_params=_params("parallel", "arbitrary"),
        name="proj_res",
    )(h, *xs, *ws)


def _gla_kernel(q_ref, k_ref, v_ref, og_ref, lr_ref, w2_ref, b_ref, gn_ref, o_ref, st_ref, *, tb):
    t = pl.program_id(1)

    @pl.when(t == 0)
    def _():
        st_ref[...] = jnp.zeros_like(st_ref)

    tril = (_iota2((CHUNK, CHUNK), 0) >= _iota2((CHUNK, CHUNK), 1))
    tril_f = tril.astype(F32)
    w2 = w2_ref[...]
    bias = b_ref[...]
    gn = gn_ref[...]
    for c in range(tb // CHUNK):
        rows = slice(c * CHUNK, (c + 1) * CHUNK)
        q = q_ref[0, rows, :] * (GLA_DK ** -0.5)
        k = k_ref[0, rows, :]
        v = jnp.concatenate([v_ref[0, rows, :], v_ref[1, rows, :]], axis=1)
        og = jnp.concatenate([og_ref[0, rows, :], og_ref[1, rows, :]], axis=1)
        z = _dot(lr_ref[rows, :], w2) + bias
        log_a = -_softplus(-z) / GLA_GATE_NORMALIZER
        cum = _dot_f32(tril_f, log_a)
        mid = cum[CHUNK // 2 - 1:CHUNK // 2, :]
        last = cum[CHUNK - 1:CHUNK, :]
        st = st_ref[...]
        scores = _dot_nt(q * jnp.exp(cum - mid), k * jnp.exp(mid - cum))
        scores = jnp.where(tril, scores, 0.0)
        out = _dot(scores, v) + _dot_nt(q * jnp.exp(cum), st)
        st_ref[...] = jnp.exp(last) * st + _dot_tn(v, k * jnp.exp(last - cum))
        o_ref[rows, :] = (_rms(out, gn) * _silu(og)).astype(o_ref.dtype)


def _gla(slabs, extra, w2, bias, gnorm, tb=256):
    s = slabs.shape[1]
    nk = GLA_KEY // LANE
    return pl.pallas_call(
        functools.partial(_gla_kernel, tb=tb),
        grid=(GLA_HEADS, s // tb),
        in_specs=[
            pl.BlockSpec((1, tb, LANE), lambda h, t: (h, t, 0)),
            pl.BlockSpec((1, tb, LANE), lambda h, t: (nk + h, t, 0)),
            pl.BlockSpec((2, tb, LANE), lambda h, t: (nk + h, t, 0)),
            pl.BlockSpec((2, tb, LANE), lambda h, t: (2 * nk + h, t, 0)),
            pl.BlockSpec((tb, LANE), lambda h, t: (t, 0)),
            pl.BlockSpec((LANE, GLA_DK), lambda h, t: (0, h)),
            pl.BlockSpec((1, GLA_DK), lambda h, t: (0, h)),
            pl.BlockSpec((1, GLA_DV), lambda h, t: (0, 0)),
        ],
        out_specs=pl.BlockSpec((tb, GLA_DV), lambda h, t: (t, h)),
        out_shape=jax.ShapeDtypeStruct((s, GLA_VAL), BF16),
        scratch_shapes=[pltpu.VMEM((GLA_DV, GLA_DK), F32)],
        compiler_params=_params("parallel", "arbitrary"),
        name="gla",
    )(slabs, slabs, slabs, slabs, extra, w2, bias, gnorm)


def _causal_conv(hist_ref, x, w, tb):
    hist_ref[CONV_PAD:CONV_PAD + tb, :] = x
    acc = w[CONV_WIDTH - 1:CONV_WIDTH, :] * x
    for kk in range(CONV_WIDTH - 1):
        off = CONV_PAD - (CONV_WIDTH - 1) + kk
        acc = acc + w[kk:kk + 1, :] * hist_ref[off:off + tb, :]
    hist_ref[0:CONV_PAD, :] = hist_ref[tb:tb + CONV_PAD, :]
    return acc


def _lru_kernel(x_ref, y_ref, cw_ref, cb_ref, wa_ref, ba_ref, wx_ref, bx_ref, lam_ref,
                o_ref, hist_ref, h_ref, *, tb):
    t = pl.program_id(1)

    @pl.when(t == 0)
    def _():
        hist_ref[0:CONV_PAD, :] = jnp.zeros((CONV_PAD, LANE), F32)
        h_ref[...] = jnp.zeros_like(h_ref)

    xc = _causal_conv(hist_ref, x_ref[0], cw_ref[...], tb) + cb_ref[...]
    r = jax.nn.sigmoid(_dot(xc, wa_ref[0]) + ba_ref[...])
    i = jax.nn.sigmoid(_dot(xc, wx_ref[0]) + bx_ref[...])
    log_a = -LRU_C * r * _softplus(-lam_ref[...])
    a = jnp.exp(log_a)
    u = jnp.sqrt(-jnp.tanh(log_a) * (a * a + 1.0)) * (i * xc)

    row = _iota2((tb, LANE), 0)
    sh = 1
    while sh < tb:
        keep = row >= sh
        a_prev = jnp.where(keep, pltpu.roll(a, sh, 0), 1.0)
        u_prev = jnp.where(keep, pltpu.roll(u, sh, 0), 0.0)
        u = u + a * u_prev
        a = a * a_prev
        sh *= 2
    h = u + a * h_ref[0:1, :]
    h_ref[...] = jnp.broadcast_to(h[tb - 1:tb, :], h_ref.shape)
    o_ref[...] = (h * jax.nn.gelu(y_ref[0])).astype(o_ref.dtype)


def _lru(slabs, x_slab0, y_slab0, conv_w, conv_b, w_a, b_a, w_x, b_x, lam, tb=512):
    s = slabs.shape[1]
    row = lambda n, t: (0, n)
    return pl.pallas_call(
        functools.partial(_lru_kernel, tb=tb),
        grid=(LRU_BLOCKS, s // tb),
        in_specs=[
            pl.BlockSpec((1, tb, LANE), lambda n, t: (x_slab0 + n, t, 0)),
            pl.BlockSpec((1, tb, LANE), lambda n, t: (y_slab0 + n, t, 0)),
            pl.BlockSpec((CONV_WIDTH, LANE), row),
            pl.BlockSpec((1, LANE), row),
            pl.BlockSpec((1, LANE, LANE), lambda n, t: (n, 0, 0)),
            pl.BlockSpec((1, LANE), row),
            pl.BlockSpec((1, LANE, LANE), lambda n, t: (n, 0, 0)),
            pl.BlockSpec((1, LANE), row),
            pl.BlockSpec((1, LANE), row),
        ],
        out_specs=pl.BlockSpec((tb, LANE), lambda n, t: (t, n)),
        out_shape=jax.ShapeDtypeStruct((s, LRU_WIDTH), BF16),
        scratch_shapes=[pltpu.VMEM((tb + CONV_PAD, LANE), F32), pltpu.VMEM((8, LANE), F32)],
        compiler_params=_params("parallel", "arbitrary"),
        name="rg_lru",
    )(slabs, slabs, conv_w, conv_b.reshape(1, -1), w_a, b_a.reshape(1, -1), w_x,
      b_x.reshape(1, -1), lam.reshape(1, -1))


def _gdn_kernel(q_ref, k_ref, v_ref, z_ref, e_ref, cwq_ref, cwk_ref, cwv_ref, alog_ref, dtb_ref,
                on_ref, o_ref, qh_ref, kh_ref, vh_ref, st_ref, *, tb):
    hq = pl.program_id(0)
    t = pl.program_id(1)
    rep = GDN_V_HEADS // GDN_QK_HEADS

    @pl.when(t == 0)
    def _():
        qh_ref[0:CONV_PAD, :] = jnp.zeros((CONV_PAD, LANE), F32)
        kh_ref[0:CONV_PAD, :] = jnp.zeros((CONV_PAD, LANE), F32)
        vh_ref[0:CONV_PAD, :] = jnp.zeros((CONV_PAD, rep * LANE), F32)
        st_ref[...] = jnp.zeros_like(st_ref)

    q_all = _silu(_causal_conv(qh_ref, q_ref[0], cwq_ref[...], tb))
    k_all = _silu(_causal_conv(kh_ref, k_ref[0], cwk_ref[...], tb))
    v_raw = jnp.concatenate([v_ref[e] for e in range(rep)], axis=1)
    v_all = _silu(_causal_conv(vh_ref, v_raw, cwv_ref[...], tb))
    q_all = q_all * lax.rsqrt(jnp.sum(q_all * q_all, axis=-1, keepdims=True) + L2_EPS) * (GDN_DK ** -0.5)
    k_all = k_all * lax.rsqrt(jnp.sum(k_all * k_all, axis=-1, keepdims=True) + L2_EPS)

    ex = e_ref[...]
    lane = _iota2((tb, LANE), 1)
    gates = jnp.where(lane < GDN_V_HEADS, jax.nn.sigmoid(ex),
                      -jnp.exp(alog_ref[...]) * _softplus(ex + dtb_ref[...]))

    ri = _iota2((CHUNK, CHUNK), 0)
    ci = _iota2((CHUNK, CHUNK), 1)
    incl = ri >= ci
    strict = ri > ci
    tril_f = incl.astype(F32)
    triu_f = (ri <= ci).astype(F32)
    ones_f = jnp.ones((CHUNK, CHUNK), F32)
    eye_f = (ri == ci).astype(F32)
    on = on_ref[...]

    for e in range(rep):
        hv = hq * rep + e
        beta_all = jnp.sum(jnp.where(lane == hv, gates, 0.0), axis=1, keepdims=True)
        g_all = jnp.sum(jnp.where(lane == hv + GDN_V_HEADS, gates, 0.0), axis=1, keepdims=True)
        for c in range(tb // CHUNK):
            rows = slice(c * CHUNK, (c + 1) * CHUNK)
            q = q_all[rows, :]
            k = k_all[rows, :]
            v = v_all[rows, e * LANE:(e + 1) * LANE]
            beta = beta_all[rows, :]
            gb = jnp.broadcast_to(g_all[rows, :], (CHUNK, CHUNK))
            ccum = _dot_f32(tril_f, gb)
            rcum = _dot_f32(ones_f, gb * triu_f)
            decay = jnp.where(incl, jnp.exp(jnp.where(incl, ccum - rcum, 0.0)), 0.0)
            gcum = ccum[:, 0:1]
            glast = ccum[CHUNK - 1:CHUNK, 0:1]
            kb = k * beta
            lmat = jnp.where(strict, _dot_nt(kb, k) * decay, 0.0)
            pw = -lmat
            tinv = eye_f + pw
            for _ in range(5):
                pw = _dot(pw, pw)
                tinv = tinv + _dot(tinv, pw)
            rhs = jnp.concatenate([v * beta, kb * jnp.exp(gcum)], axis=1)
            sol = _dot(tinv, rhs)
            st = st_ref[e]
            v_new = sol[:, :GDN_DV] - _dot(sol[:, GDN_DV:], st)
            attn = jnp.where(incl, _dot_nt(q, k) * decay, 0.0)
            out = _dot(q * jnp.exp(gcum), st) + _dot(attn, v_new)
            st_ref[e] = jnp.exp(glast) * st + _dot_tn(k * jnp.exp(glast - gcum), v_new)
            zg = z_ref[e, rows, :]
            o_ref[rows, e * LANE:(e + 1) * LANE] = (_rms(out, on) * _silu(zg)).astype(o_ref.dtype)


def _gdn(slabs, extra, conv_w, a_log, dt_bias, out_norm, tb=256):
    s = slabs.shape[1]
    rep = GDN_V_HEADS // GDN_QK_HEADS
    nq = GDN_KEY // LANE
    alog_row = jnp.zeros((1, LANE), F32).at[0, GDN_V_HEADS:2 * GDN_V_HEADS].set(a_log)
    dtb_row = jnp.zeros((1, LANE), F32).at[0, GDN_V_HEADS:2 * GDN_V_HEADS].set(dt_bias)
    return pl.pallas_call(
        functools.partial(_gdn_kernel, tb=tb),
        grid=(GDN_QK_HEADS, s // tb),
        in_specs=[
            pl.BlockSpec((1, tb, LANE), lambda h, t: (h, t, 0)),
            pl.BlockSpec((1, tb, LANE), lambda h, t: (nq + h, t, 0)),
            pl.BlockSpec((rep, tb, LANE), lambda h, t: (nq + h, t, 0)),
            pl.BlockSpec((rep, tb, LANE), lambda h, t: (2 * nq + h, t, 0)),
            pl.BlockSpec((tb, LANE), lambda h, t: (t, 0)),
            pl.BlockSpec((CONV_WIDTH, LANE), lambda h, t: (0, h)),
            pl.BlockSpec((CONV_WIDTH, LANE), lambda h, t: (0, nq + h)),
            pl.BlockSpec((CONV_WIDTH, rep * LANE), lambda h, t: (0, nq + h)),
            pl.BlockSpec((1, LANE), lambda h, t: (0, 0)),
            pl.BlockSpec((1, LANE), lambda h, t: (0, 0)),
            pl.BlockSpec((1, GDN_DV), lambda h, t: (0, 0)),
        ],
        out_specs=pl.BlockSpec((tb, rep * LANE), lambda h, t: (t, h)),
        out_shape=jax.ShapeDtypeStruct((s, GDN_VAL), BF16),
        scratch_shapes=[
            pltpu.VMEM((tb + CONV_PAD, LANE), F32),
            pltpu.VMEM((tb + CONV_PAD, LANE), F32),
            pltpu.VMEM((tb + CONV_PAD, rep * LANE), F32),
            pltpu.VMEM((rep, GDN_DK, GDN_DV), F32),
        ],
        compiler_params=_params("parallel", "arbitrary"),
        name="gdn",
    )(slabs, slabs, slabs, slabs, extra, conv_w, conv_w, conv_w, alog_row, dtb_row,
      out_norm.reshape(1, -1))


def _pad_cols(w, n):
    return jnp.pad(w, ((0, 0), (0, n - w.shape[1])))


def _mixer_a(h, gain, w_in, gate_w2, gate_b, gla_norm, conv_w, conv_b, w_a, b_a, w_x, b_x, lam, w_out):
    c_lr = 2 * GLA_KEY + 2 * GLA_VAL
    c_x = c_lr + GLA_GATE_RANK
    w_main = jnp.concatenate([w_in[:, :c_lr], w_in[:, c_x:]], axis=1).astype(BF16)
    w_extra = _pad_cols(w_in[:, c_lr:c_x], LANE).astype(BF16)
    slabs, extra = _norm_proj(h, gain, w_main, w_extra)
    w2 = jnp.pad(gate_w2, ((0, LANE - GLA_GATE_RANK), (0, 0)))
    o_gla = _gla(slabs, extra, w2, gate_b.reshape(1, -1), gla_norm.reshape(1, -1))
    x_slab0 = c_lr // LANE
    o_lru = _lru(slabs, x_slab0, x_slab0 + LRU_WIDTH // LANE, conv_w, conv_b, w_a, b_a, w_x, b_x, lam)
    w_out = w_out.astype(BF16)
    return _proj_res(h, [o_gla, o_lru], [w_out[:GLA_VAL], w_out[GLA_VAL:]])


def _mixer_c(h, gain, w_in, conv_w, a_log, dt_bias, out_norm, w_out):
    c_main = 2 * GDN_KEY + 2 * GDN_VAL
    w_main = w_in[:, :c_main].astype(BF16)
    w_extra = _pad_cols(w_in[:, c_main:], LANE).astype(BF16)
    slabs, extra = _norm_proj(h, gain, w_main, w_extra)
    o = _gdn(slabs, extra, conv_w, a_log, dt_bias, out_norm)
    return _proj_res(h, [o], [w_out.astype(BF16)])


def kernel(x, norms, ffn_w_gate, ffn_w_up, ffn_w_down, a_w_in, a_gla_gate_w2, a_gla_gate_b, a_gla_norm, a_rg_conv_w, a_rg_conv_b, a_rg_w_a, a_rg_b_a, a_rg_w_x, a_rg_b_x, a_rg_lambda, a_w_out, c_w_in, c_conv_w, c_a_log, c_dt_bias, c_out_norm, c_w_out, final_norm):
    b, s, d = x.shape
    depth = norms.shape[0]
    wg = ffn_w_gate.astype(BF16)
    wu = ffn_w_up.astype(BF16)
    wd = ffn_w_down.astype(BF16)
    outs = []
    for bi in range(b):
        h = x[bi]
        for layer in range(depth):
            j = layer // 2
            h = _ffn(h, norms[layer, 0], wg[layer, 0], wu[layer, 0], wd[layer, 0])
            if layer % 2 == 0:
                h = _mixer_a(h, norms[layer, 1], a_w_in[j], a_gla_gate_w2[j], a_gla_gate_b[j],
                             a_gla_norm[j], a_rg_conv_w[j], a_rg_conv_b[j], a_rg_w_a[j], a_rg_b_a[j],
                             a_rg_w_x[j], a_rg_b_x[j], a_rg_lambda[j], a_w_out[j])
            else:
                h = _mixer_c(h, norms[layer, 1], c_w_in[j], c_conv_w[j], c_a_log[j], c_dt_bias[j],
                             c_out_norm[j], c_w_out[j])
            last = layer == depth - 1
            h = _ffn(h, norms[layer, 2], wg[layer, 1], wu[layer, 1], wd[layer, 1],
                     final_gain=final_norm if last else None)
        outs.append(h)
    return jnp.stack(outs, axis=0)
```

```python
import functools

import jax
import jax.numpy as jnp
from jax import lax
from jax.experimental import pallas as pl
from jax.experimental.pallas import tpu as pltpu

F32 = jnp.float32
BF16 = jnp.bfloat16
HIGHEST = lax.Precision.HIGHEST

LANE = 128
D_MODEL = 2048
D_FF = 5632
RMS_EPS = 1e-6
L2_EPS = 1e-6
CONV_WIDTH = 4
CONV_PAD = 8

GLA_HEADS = 4
GLA_DK = 128
GLA_DV = 256
GLA_KEY = GLA_HEADS * GLA_DK
GLA_VAL = GLA_HEADS * GLA_DV
GLA_GATE_RANK = 16
GLA_GATE_NORMALIZER = 16.0
CHUNK = 64

LRU_WIDTH = 1024
LRU_BLOCKS = 8
LRU_C = 8.0

GDN_QK_HEADS = 16
GDN_V_HEADS = 32
GDN_REP = GDN_V_HEADS // GDN_QK_HEADS
GDN_GROUP = 2
GDN_DK = 128
GDN_DV = 128
GDN_KEY = GDN_QK_HEADS * GDN_DK
GDN_VAL = GDN_V_HEADS * GDN_DV

VMEM_LIMIT = 56 * 1024 * 1024


def _params(*sem):
    return pltpu.CompilerParams(dimension_semantics=sem, vmem_limit_bytes=VMEM_LIMIT)


def _rms(x, gain):
    return x * lax.rsqrt(jnp.mean(x * x, axis=-1, keepdims=True) + RMS_EPS) * gain


def _softplus(z):
    return jnp.maximum(z, 0.0) + jnp.log1p(jnp.exp(-jnp.abs(z)))


def _silu(z):
    return z * jax.nn.sigmoid(z)


def _dot(a, b):
    return jnp.dot(a.astype(BF16), b.astype(BF16), preferred_element_type=F32)


def _dot_nt(a, b):
    return lax.dot_general(a.astype(BF16), b.astype(BF16), (((1,), (1,)), ((), ())),
                           preferred_element_type=F32)


def _dot_tn(a, b):
    return lax.dot_general(a.astype(BF16), b.astype(BF16), (((0,), (0,)), ((), ())),
                           preferred_element_type=F32)


def _dot_f32(a, b):
    return jnp.dot(a, b, precision=HIGHEST, preferred_element_type=F32)


def _iota2(shape, dim):
    return lax.broadcasted_iota(jnp.int32, shape, dim)


def _ffn_kernel(x_ref, g_ref, wg_ref, wu_ref, wd_ref, fg_ref, o_ref, n_ref, *, final_norm):
    j = pl.program_id(1)

    @pl.when(j == 0)
    def _():
        x = x_ref[...]
        n_ref[...] = _rms(x, g_ref[...]).astype(BF16)
        o_ref[...] = x

    n = n_ref[...]
    gate = jnp.dot(n, wg_ref[...], preferred_element_type=F32)
    up = jnp.dot(n, wu_ref[...], preferred_element_type=F32)
    act = (0.5 * _silu(gate) * up).astype(BF16)
    o_ref[...] += jnp.dot(act, wd_ref[...], preferred_element_type=F32)

    if final_norm:
        @pl.when(j == pl.num_programs(1) - 1)
        def _():
            o_ref[...] = _rms(o_ref[...], fg_ref[...])


def _ffn(h, gain, wg, wu, wd, final_gain=None, tm=512, tf=512):
    s, d = h.shape
    dff = wg.shape[1]
    fg = jnp.ones((1, d), F32) if final_gain is None else final_gain.reshape(1, d)
    return pl.pallas_call(
        functools.partial(_ffn_kernel, final_norm=final_gain is not None),
        grid=(s // tm, dff // tf),
        in_specs=[
            pl.BlockSpec((tm, d), lambda i, j: (i, 0)),
            pl.BlockSpec((1, d), lambda i, j: (0, 0)),
            pl.BlockSpec((d, tf), lambda i, j: (0, j)),
            pl.BlockSpec((d, tf), lambda i, j: (0, j)),
            pl.BlockSpec((tf, d), lambda i, j: (j, 0)),
            pl.BlockSpec((1, d), lambda i, j: (0, 0)),
        ],
        out_specs=pl.BlockSpec((tm, d), lambda i, j: (i, 0)),
        out_shape=jax.ShapeDtypeStruct((s, d), F32),
        scratch_shapes=[pltpu.VMEM((tm, d), BF16)],
        compiler_params=_params("parallel", "arbitrary"),
        name="ffn",
    )(h, gain.reshape(1, d), wg, wu, wd, fg)


def _norm_proj_kernel(x_ref, g_ref, w_ref, we_ref, o_ref, e_ref, n_ref):
    j = pl.program_id(1)

    @pl.when(j == 0)
    def _():
        nb = _rms(x_ref[...], g_ref[...]).astype(BF16)
        n_ref[...] = nb
        e_ref[...] = jnp.dot(nb, we_ref[...], preferred_element_type=F32)

    acc = jnp.dot(n_ref[...], w_ref[...], preferred_element_type=F32)
    for c in range(o_ref.shape[0]):
        o_ref[c] = acc[:, c * LANE:(c + 1) * LANE]


def _norm_proj(h, gain, w_main, w_extra, tm=512, tn=1024):
    s, d = h.shape
    n = w_main.shape[1]
    return pl.pallas_call(
        _norm_proj_kernel,
        grid=(s // tm, n // tn),
        in_specs=[
            pl.BlockSpec((tm, d), lambda i, j: (i, 0)),
            pl.BlockSpec((1, d), lambda i, j: (0, 0)),
            pl.BlockSpec((d, tn), lambda i, j: (0, j)),
            pl.BlockSpec((d, LANE), lambda i, j: (0, 0)),
        ],
        out_specs=[
            pl.BlockSpec((tn // LANE, tm, LANE), lambda i, j: (j, i, 0)),
            pl.BlockSpec((tm, LANE), lambda i, j: (i, 0)),
        ],
        out_shape=[
            jax.ShapeDtypeStruct((n // LANE, s, LANE), F32),
            jax.ShapeDtypeStruct((s, LANE), F32),
        ],
        scratch_shapes=[pltpu.VMEM((tm, d), BF16)],
        compiler_params=_params("parallel", "arbitrary"),
        name="norm_proj",
    )(h, gain.reshape(1, d), w_main, w_extra)


def _proj_res_kernel(*refs, n_in):
    h_ref = refs[0]
    x_refs = refs[1:1 + n_in]
    w_refs = refs[1 + n_in:1 + 2 * n_in]
    o_ref = refs[1 + 2 * n_in]
    acc = h_ref[...]
    for x_ref, w_ref in zip(x_refs, w_refs):
        acc = acc + jnp.dot(x_ref[...], w_ref[...], preferred_element_type=F32)
    o_ref[...] = acc


def _proj_res(h, xs, ws, tm=512, tn=512):
    s, d = h.shape
    n_in = len(xs)
    in_specs = [pl.BlockSpec((tm, tn), lambda i, j: (i, j))]
    in_specs += [pl.BlockSpec((tm, x.shape[1]), lambda i, j: (i, 0)) for x in xs]
    in_specs += [pl.BlockSpec((w.shape[0], tn), lambda i, j: (0, j)) for w in ws]
    return pl.pallas_call(
        functools.partial(_proj_res_kernel, n_in=n_in),
        grid=(s // tm, d // tn),
        in_specs=in_specs,
        out_specs=pl.BlockSpec((tm, tn), lambda i, j: (i, j)),
        out_shape=jax.ShapeDtypeStruct((s, d), F32),
        compiler_params=_params("parallel", "arbitrary"),
        name="proj_res",
    )(h, *xs, *ws)


def _gla_kernel(q_ref, k_ref, v_ref, og_ref, lr_ref, w2_ref, b_ref, gn_ref, o_ref, st_ref, *, tb):
    t = pl.program_id(1)

    @pl.when(t == 0)
    def _():
        st_ref[...] = jnp.zeros_like(st_ref)

    tril = (_iota2((CHUNK, CHUNK), 0) >= _iota2((CHUNK, CHUNK), 1))
    tril_f = tril.astype(F32)
    w2 = w2_ref[...]
    bias = b_ref[...]
    gn = gn_ref[...]
    for c in range(tb // CHUNK):
        rows = slice(c * CHUNK, (c + 1) * CHUNK)
        q = q_ref[0, rows, :] * (GLA_DK ** -0.5)
        k = k_ref[0, rows, :]
        v = jnp.concatenate([v_ref[0, rows, :], v_ref[1, rows, :]], axis=1)
        og = jnp.concatenate([og_ref[0, rows, :], og_ref[1, rows, :]], axis=1)
        z = _dot(lr_ref[rows, :], w2) + bias
        log_a = -_softplus(-z) / GLA_GATE_NORMALIZER
        cum = _dot_f32(tril_f, log_a)
        mid = cum[CHUNK // 2 - 1:CHUNK // 2, :]
        last = cum[CHUNK - 1:CHUNK, :]
        st = st_ref[...]
        scores = _dot_nt(q * jnp.exp(cum - mid), k * jnp.exp(mid - cum))
        scores = jnp.where(tril, scores, 0.0)
        out = _dot(scores, v) + _dot_nt(q * jnp.exp(cum), st)
        st_ref[...] = jnp.exp(last) * st + _dot_tn(v, k * jnp.exp(last - cum))
        o_ref[rows, :] = (_rms(out, gn) * _silu(og)).astype(o_ref.dtype)


def _gla(slabs, extra, w2, bias, gnorm, tb=256):
    s = slabs.shape[1]
    nk = GLA_KEY // LANE
    return pl.pallas_call(
        functools.partial(_gla_kernel, tb=tb),
        grid=(GLA_HEADS, s // tb),
        in_specs=[
            pl.BlockSpec((1, tb, LANE), lambda h, t: (h, t, 0)),
            pl.BlockSpec((1, tb, LANE), lambda h, t: (nk + h, t, 0)),
            pl.BlockSpec((2, tb, LANE), lambda h, t: (nk + h, t, 0)),
            pl.BlockSpec((2, tb, LANE), lambda h, t: (2 * nk + h, t, 0)),
            pl.BlockSpec((tb, LANE), lambda h, t: (t, 0)),
            pl.BlockSpec((LANE, GLA_DK), lambda h, t: (0, h)),
            pl.BlockSpec((1, GLA_DK), lambda h, t: (0, h)),
            pl.BlockSpec((1, GLA_DV), lambda h, t: (0, 0)),
        ],
        out_specs=pl.BlockSpec((tb, GLA_DV), lambda h, t: (t, h)),
        out_shape=jax.ShapeDtypeStruct((s, GLA_VAL), BF16),
        scratch_shapes=[pltpu.VMEM((GLA_DV, GLA_DK), F32)],
        compiler_params=_params("parallel", "arbitrary"),
        name="gla",
    )(slabs, slabs, slabs, slabs, extra, w2, bias, gnorm)


def _causal_conv(hist_ref, x, w, tb):
    hist_ref[CONV_PAD:CONV_PAD + tb, :] = x
    acc = w[CONV_WIDTH - 1:CONV_WIDTH, :] * x
    for kk in range(CONV_WIDTH - 1):
        off = CONV_PAD - (CONV_WIDTH - 1) + kk
        acc = acc + w[kk:kk + 1, :] * hist_ref[off:off + tb, :]
    hist_ref[0:CONV_PAD, :] = hist_ref[tb:tb + CONV_PAD, :]
    return acc


def _lru_kernel(x_ref, y_ref, cw_ref, cb_ref, wa_ref, ba_ref, wx_ref, bx_ref, lam_ref,
                o_ref, hist_ref, h_ref, *, tb):
    t = pl.program_id(1)

    @pl.when(t == 0)
    def _():
        hist_ref[0:CONV_PAD, :] = jnp.zeros((CONV_PAD, LANE), F32)
        h_ref[...] = jnp.zeros_like(h_ref)

    xc = _causal_conv(hist_ref, x_ref[0], cw_ref[...], tb) + cb_ref[...]
    r = jax.nn.sigmoid(_dot(xc, wa_ref[0]) + ba_ref[...])
    i = jax.nn.sigmoid(_dot(xc, wx_ref[0]) + bx_ref[...])
    log_a = -LRU_C * r * _softplus(-lam_ref[...])
    a = jnp.exp(log_a)
    u = jnp.sqrt(-jnp.tanh(log_a) * (a * a + 1.0)) * (i * xc)

    row = _iota2((tb, LANE), 0)
    sh = 1
    while sh < tb:
        keep = row >= sh
        a_prev = jnp.where(keep, pltpu.roll(a, sh, 0), 1.0)
        u_prev = jnp.where(keep, pltpu.roll(u, sh, 0), 0.0)
        u = u + a * u_prev
        a = a * a_prev
        sh *= 2
    h = u + a * h_ref[0:1, :]
    h_ref[...] = jnp.broadcast_to(h[tb - 1:tb, :], h_ref.shape)
    o_ref[...] = (h * jax.nn.gelu(y_ref[0])).astype(o_ref.dtype)


def _lru(slabs, x_slab0, y_slab0, conv_w, conv_b, w_a, b_a, w_x, b_x, lam, tb=512):
    s = slabs.shape[1]
    row = lambda n, t: (0, n)
    return pl.pallas_call(
        functools.partial(_lru_kernel, tb=tb),
        grid=(LRU_BLOCKS, s // tb),
        in_specs=[
            pl.BlockSpec((1, tb, LANE), lambda n, t: (x_slab0 + n, t, 0)),
            pl.BlockSpec((1, tb, LANE), lambda n, t: (y_slab0 + n, t, 0)),
            pl.BlockSpec((CONV_WIDTH, LANE), row),
            pl.BlockSpec((1, LANE), row),
            pl.BlockSpec((1, LANE, LANE), lambda n, t: (n, 0, 0)),
            pl.BlockSpec((1, LANE), row),
            pl.BlockSpec((1, LANE, LANE), lambda n, t: (n, 0, 0)),
            pl.BlockSpec((1, LANE), row),
            pl.BlockSpec((1, LANE), row),
        ],
        out_specs=pl.BlockSpec((tb, LANE), lambda n, t: (t, n)),
        out_shape=jax.ShapeDtypeStruct((s, LRU_WIDTH), BF16),
        scratch_shapes=[pltpu.VMEM((tb + CONV_PAD, LANE), F32), pltpu.VMEM((8, LANE), F32)],
        compiler_params=_params("parallel", "arbitrary"),
        name="rg_lru",
    )(slabs, slabs, conv_w, conv_b.reshape(1, -1), w_a, b_a.reshape(1, -1), w_x,
      b_x.reshape(1, -1), lam.reshape(1, -1))


def _gdn_gates_kernel(e_ref, alog_ref, dtb_ref, cols_ref, rows_ref, *, tb):
    ex = e_ref[...]
    is_beta = _iota2((CHUNK, LANE), 1) < GDN_V_HEADS
    beta = jax.nn.sigmoid(ex)
    g = -jnp.exp(alog_ref[...]) * _softplus(ex + dtb_ref[...])
    tril_f = (_iota2((CHUNK, CHUNK), 0) >= _iota2((CHUNK, CHUNK), 1)).astype(F32)
    for c in range(tb // CHUNK):
        rows = slice(c * CHUNK, (c + 1) * CHUNK)
        gcum = _dot_f32(tril_f, g[rows, :])
        cols_ref[rows, :] = jnp.where(is_beta, beta[rows, :], gcum)
        rows_ref[c] = gcum.T


def _gdn_gates(extra, a_log, dt_bias, tb=512):
    s = extra.shape[0]
    alog_row = jnp.zeros((1, LANE), F32).at[0, GDN_V_HEADS:2 * GDN_V_HEADS].set(a_log)
    dtb_row = jnp.zeros((1, LANE), F32).at[0, GDN_V_HEADS:2 * GDN_V_HEADS].set(dt_bias)
    return pl.pallas_call(
        functools.partial(_gdn_gates_kernel, tb=tb),
        grid=(s // tb,),
        in_specs=[
            pl.BlockSpec((tb, LANE), lambda t: (t, 0)),
            pl.BlockSpec((1, LANE), lambda t: (0, 0)),
            pl.BlockSpec((1, LANE), lambda t: (0, 0)),
        ],
        out_specs=[
            pl.BlockSpec((tb, LANE), lambda t: (t, 0)),
            pl.BlockSpec((tb // CHUNK, LANE, CHUNK), lambda t: (t, 0, 0)),
        ],
        out_shape=[
            jax.ShapeDtypeStruct((s, LANE), F32),
            jax.ShapeDtypeStruct((s // CHUNK, LANE, CHUNK), F32),
        ],
        compiler_params=_params("parallel"),
        name="gdn_gates",
    )(extra, alog_row, dtb_row)


def _gdn_kernel(q_ref, k_ref, v_ref, z_ref, cols_ref, rows_ref, cwq_ref, cwk_ref, cwv_ref,
                on_ref, o_ref, qh_ref, kh_ref, vh_ref, st_ref, *, tb):
    hg = pl.program_id(0)
    t = pl.program_id(1)
    heads = range(GDN_GROUP)

    @pl.when(t == 0)
    def _():
        qh_ref[0:CONV_PAD, :] = jnp.zeros((CONV_PAD, qh_ref.shape[1]), F32)
        kh_ref[0:CONV_PAD, :] = jnp.zeros((CONV_PAD, kh_ref.shape[1]), F32)
        vh_ref[0:CONV_PAD, :] = jnp.zeros((CONV_PAD, vh_ref.shape[1]), F32)
        st_ref[...] = jnp.zeros_like(st_ref)

    def conv_silu(hist_ref, x_ref, w_ref):
        raw = jnp.concatenate([x_ref[i] for i in range(x_ref.shape[0])], axis=1)
        return _silu(_causal_conv(hist_ref, raw, w_ref[...], tb))

    def l2n(x):
        return x * lax.rsqrt(jnp.sum(x * x, axis=-1, keepdims=True) + L2_EPS)

    q_conv = conv_silu(qh_ref, q_ref, cwq_ref)
    k_conv = conv_silu(kh_ref, k_ref, cwk_ref)
    v_all = conv_silu(vh_ref, v_ref, cwv_ref)
    q_all = [l2n(q_conv[:, j * LANE:(j + 1) * LANE]) * (GDN_DK ** -0.5) for j in heads]
    k_all = [l2n(k_conv[:, j * LANE:(j + 1) * LANE]) for j in heads]

    lane = _iota2((CHUNK, LANE), 1)
    sub = _iota2((CHUNK, LANE), 0)
    left = lane < CHUNK
    col = jnp.where(left, lane, lane - CHUNK)
    incl = sub >= col
    strict = sub > col
    eye_f = (_iota2((LANE, LANE), 0) == _iota2((LANE, LANE), 1)).astype(F32)
    on = on_ref[...]

    def block_diag(m):
        return jnp.concatenate([jnp.where(left, m, 0.0), jnp.where(left, 0.0, m)], axis=0)

    chunks = range(tb // CHUNK)
    units = [(j, c) for c in chunks for j in heads]
    vheads = range(GDN_REP)
    rows = [slice(c * CHUNK, (c + 1) * CHUNK) for c in chunks]
    q, k, beta, gcum, glast, xpw, tinv, attn = {}, {}, {}, {}, {}, {}, {}, {}
    for u in units:
        j, c = u
        hq = hg * GDN_GROUP + j
        q[u] = q_all[j][rows[c], :]
        k[u] = k_all[j][rows[c], :]
        cols = cols_ref[rows[c], :]
        hv = [hq * GDN_REP + e for e in vheads]
        beta[u] = [jnp.sum(jnp.where(lane == h, cols, 0.0), axis=1, keepdims=True) for h in hv]
        gcum[u] = [jnp.sum(jnp.where(lane == h + GDN_V_HEADS, cols, 0.0), axis=1, keepdims=True)
                   for h in hv]
        grow_p = rows_ref[c, pl.ds(hq + GDN_V_HEADS // GDN_REP, 1), :]
        glast[u] = [grow_p[:, (e + 1) * CHUNK - 1:(e + 1) * CHUNK] for e in vheads]
        gcol_p = jnp.where(left, gcum[u][0], gcum[u][1])
        decay = jnp.where(incl, jnp.exp(jnp.where(incl, gcol_p - grow_p, 0.0)), 0.0)
        kq = _dot_nt(jnp.concatenate([k[u], q[u]], axis=0), jnp.concatenate([k[u], k[u]], axis=0))
        beta_p = jnp.where(left, beta[u][0], beta[u][1])
        lmat = jnp.where(strict, kq[:CHUNK] * beta_p * decay, 0.0)
        attn[u] = kq[CHUNK:] * decay
        xpw[u] = block_diag(-lmat)
        tinv[u] = eye_f + xpw[u]
    for u in units:
        xpw[u] = _dot(xpw[u], xpw[u])
    for _ in range(4):
        for u in units:
            both = _dot(jnp.concatenate([tinv[u], xpw[u]], axis=0), xpw[u])
            tinv[u] = tinv[u] + both[:LANE]
            xpw[u] = both[LANE:]
    for u in units:
        tinv[u] = tinv[u] + _dot(tinv[u], xpw[u])
    sol, att_sol = {}, {}
    for u in units:
        j, c = u
        rhs = jnp.concatenate(
            [jnp.concatenate([v_all[rows[c], (j * GDN_REP + e) * LANE:(j * GDN_REP + e + 1) * LANE]
                              * beta[u][e],
                              k[u] * (beta[u][e] * jnp.exp(gcum[u][e]))], axis=1)
             for e in vheads], axis=0)
        sol[u] = _dot(tinv[u], rhs)
    for u in units:
        att_sol[u] = _dot(block_diag(attn[u]), sol[u])
    lhs, upd = {}, {}
    for u in units:
        for e in vheads:
            hrows = slice(e * CHUNK, (e + 1) * CHUNK)
            kd = k[u] * jnp.exp(glast[u][e] - gcum[u][e])
            kd_sol = _dot_tn(kd, sol[u][hrows, :])
            qd = q[u] * jnp.exp(gcum[u][e]) - att_sol[u][hrows, GDN_DV:]
            lhs[u, e] = jnp.concatenate([kd_sol[:, GDN_DV:], qd], axis=0)
            upd[u, e] = kd_sol[:, :GDN_DV]
    for u in units:
        j, c = u
        for e in vheads:
            hrows = slice(e * CHUNK, (e + 1) * CHUNK)
            hv = j * GDN_REP + e
            st = st_ref[hv]
            prod = _dot(lhs[u, e], st)
            st_ref[hv] = jnp.exp(glast[u][e]) * st + upd[u, e] - prod[:GDN_DK]
            out = prod[GDN_DK:] + att_sol[u][hrows, :GDN_DV]
            zg = z_ref[hv, rows[c], :]
            o_ref[rows[c], hv * LANE:(hv + 1) * LANE] = (_rms(out, on) * _silu(zg)).astype(o_ref.dtype)


def _gdn(slabs, cols, rows, conv_w, out_norm, tb=256):
    s = slabs.shape[1]
    g = GDN_GROUP
    gv = GDN_GROUP * GDN_REP
    nq = GDN_QK_HEADS // g
    rows = rows.reshape(s // CHUNK, LANE // GDN_REP, GDN_REP * CHUNK)
    return pl.pallas_call(
        functools.partial(_gdn_kernel, tb=tb),
        grid=(nq, s // tb),
        in_specs=[
            pl.BlockSpec((g, tb, LANE), lambda h, t: (h, t, 0)),
            pl.BlockSpec((g, tb, LANE), lambda h, t: (nq + h, t, 0)),
            pl.BlockSpec((gv, tb, LANE), lambda h, t: (nq + h, t, 0)),
            pl.BlockSpec((gv, tb, LANE), lambda h, t: (2 * nq + h, t, 0)),
            pl.BlockSpec((tb, LANE), lambda h, t: (t, 0)),
            pl.BlockSpec((tb // CHUNK, LANE // GDN_REP, GDN_REP * CHUNK), lambda h, t: (t, 0, 0)),
            pl.BlockSpec((CONV_WIDTH, g * LANE), lambda h, t: (0, h)),
            pl.BlockSpec((CONV_WIDTH, g * LANE), lambda h, t: (0, nq + h)),
            pl.BlockSpec((CONV_WIDTH, gv * LANE), lambda h, t: (0, nq + h)),
            pl.BlockSpec((1, GDN_DV), lambda h, t: (0, 0)),
        ],
        out_specs=pl.BlockSpec((tb, gv * LANE), lambda h, t: (t, h)),
        out_shape=jax.ShapeDtypeStruct((s, GDN_VAL), BF16),
        scratch_shapes=[
            pltpu.VMEM((tb + CONV_PAD, g * LANE), F32),
            pltpu.VMEM((tb + CONV_PAD, g * LANE), F32),
            pltpu.VMEM((tb + CONV_PAD, gv * LANE), F32),
            pltpu.VMEM((gv, GDN_DK, GDN_DV), F32),
        ],
        compiler_params=_params("parallel", "arbitrary"),
        name="gdn",
    )(slabs, slabs, slabs, slabs, cols, rows, conv_w, conv_w, conv_w, out_norm.reshape(1, -1))


def _pad_cols(w, n):
    return jnp.pad(w, ((0, 0), (0, n - w.shape[1])))


def _mixer_a(h, gain, w_in, gate_w2, gate_b, gla_norm, conv_w, conv_b, w_a, b_a, w_x, b_x, lam, w_out):
    c_lr = 2 * GLA_KEY + 2 * GLA_VAL
    c_x = c_lr + GLA_GATE_RANK
    w_main = jnp.concatenate([w_in[:, :c_lr], w_in[:, c_x:]], axis=1).astype(BF16)
    w_extra = _pad_cols(w_in[:, c_lr:c_x], LANE).astype(BF16)
    slabs, extra = _norm_proj(h, gain, w_main, w_extra)
    w2 = jnp.pad(gate_w2, ((0, LANE - GLA_GATE_RANK), (0, 0)))
    o_gla = _gla(slabs, extra, w2, gate_b.reshape(1, -1), gla_norm.reshape(1, -1))
    x_slab0 = c_lr // LANE
    o_lru = _lru(slabs, x_slab0, x_slab0 + LRU_WIDTH // LANE, conv_w, conv_b, w_a, b_a, w_x, b_x, lam)
    w_out = w_out.astype(BF16)
    return _proj_res(h, [o_gla, o_lru], [w_out[:GLA_VAL], w_out[GLA_VAL:]])


def _mixer_c(h, gain, w_in, conv_w, a_log, dt_bias, out_norm, w_out):
    c_main = 2 * GDN_KEY + 2 * GDN_VAL
    w_main = w_in[:, :c_main].astype(BF16)
    w_extra = _pad_cols(w_in[:, c_main:], LANE).astype(BF16)
    slabs, extra = _norm_proj(h, gain, w_main, w_extra)
    cols, rows = _gdn_gates(extra, a_log, dt_bias)
    o = _gdn(slabs, cols, rows, conv_w, out_norm)
    return _proj_res(h, [o], [w_out.astype(BF16)])


def kernel(x, norms, ffn_w_gate, ffn_w_up, ffn_w_down, a_w_in, a_gla_gate_w2, a_gla_gate_b, a_gla_norm, a_rg_conv_w, a_rg_conv_b, a_rg_w_a, a_rg_b_a, a_rg_w_x, a_rg_b_x, a_rg_lambda, a_w_out, c_w_in, c_conv_w, c_a_log, c_dt_bias, c_out_norm, c_w_out, final_norm):
    b, s, d = x.shape
    depth = norms.shape[0]
    wg = ffn_w_gate.astype(BF16)
    wu = ffn_w_up.astype(BF16)
    wd = ffn_w_down.astype(BF16)
    outs = []
    for bi in range(b):
        h = x[bi]
        for layer in range(depth):
            j = layer // 2
            h = _ffn(h, norms[layer, 0], wg[layer, 0], wu[layer, 0], wd[layer, 0])
            if layer % 2 == 0:
                h = _mixer_a(h, norms[layer, 1], a_w_in[j], a_gla_gate_w2[j], a_gla_gate_b[j],
                             a_gla_norm[j], a_rg_conv_w[j], a_rg_conv_b[j], a_rg_w_a[j], a_rg_b_a[j],
                             a_rg_w_x[j], a_rg_b_x[j], a_rg_lambda[j], a_w_out[j])
            else:
                h = _mixer_c(h, norms[layer, 1], c_w_in[j], c_conv_w[j], c_a_log[j], c_dt_bias[j],
                             c_out_norm[j], c_w_out[j])
            last = layer == depth - 1
            h = _ffn(h, norms[layer, 2], wg[layer, 1], wu[layer, 1], wd[layer, 1],
                     final_gain=final_norm if last else None)
        outs.append(h)
    return jnp.stack(outs, axis=0)
```

```python
import functools

import jax
import jax.numpy as jnp
from jax import lax
from jax.experimental import pallas as pl
from jax.experimental.pallas import tpu as pltpu

F32 = jnp.float32
BF16 = jnp.bfloat16
HIGHEST = lax.Precision.HIGHEST

LANE = 128
D_MODEL = 2048
D_FF = 5632
RMS_EPS = 1e-6
L2_EPS = 1e-6
CONV_WIDTH = 4
CONV_PAD = 8
CONV_PAD_BF16 = 16

GLA_HEADS = 4
GLA_DK = 128
GLA_DV = 256
GLA_KEY = GLA_HEADS * GLA_DK
GLA_VAL = GLA_HEADS * GLA_DV
GLA_GATE_RANK = 16
GLA_GATE_NORMALIZER = 16.0
CHUNK = 64

LRU_WIDTH = 1024
LRU_BLOCKS = 8
LRU_C = 8.0

GDN_QK_HEADS = 16
GDN_V_HEADS = 32
GDN_REP = GDN_V_HEADS // GDN_QK_HEADS
GDN_GROUP = 2
GDN_DK = 128
GDN_DV = 128
GDN_KEY = GDN_QK_HEADS * GDN_DK
GDN_VAL = GDN_V_HEADS * GDN_DV

VMEM_LIMIT = 56 * 1024 * 1024


def _params(*sem):
    return pltpu.CompilerParams(dimension_semantics=sem, vmem_limit_bytes=VMEM_LIMIT)


def _rms(x, gain):
    return x * lax.rsqrt(jnp.mean(x * x, axis=-1, keepdims=True) + RMS_EPS) * gain


def _softplus(z):
    return jnp.maximum(z, 0.0) + jnp.log1p(jnp.exp(-jnp.abs(z)))


def _silu(z):
    return z * jax.nn.sigmoid(z)


def _dot(a, b):
    return jnp.dot(a.astype(BF16), b.astype(BF16), preferred_element_type=F32)


def _dot_nt(a, b):
    return lax.dot_general(a.astype(BF16), b.astype(BF16), (((1,), (1,)), ((), ())),
                           preferred_element_type=F32)


def _dot_tn(a, b):
    return lax.dot_general(a.astype(BF16), b.astype(BF16), (((0,), (0,)), ((), ())),
                           preferred_element_type=F32)


def _dot_f32(a, b):
    return jnp.dot(a, b, precision=HIGHEST, preferred_element_type=F32)


def _iota2(shape, dim):
    return lax.broadcasted_iota(jnp.int32, shape, dim)


def _ffn_kernel(x_ref, g_ref, wg_ref, wu_ref, wd_ref, fg_ref, o_ref, n_ref, *, final_norm):
    j = pl.program_id(1)

    @pl.when(j == 0)
    def _():
        x = x_ref[...]
        n_ref[...] = _rms(x, g_ref[...]).astype(BF16)
        o_ref[...] = x

    n = n_ref[...]
    gate = jnp.dot(n, wg_ref[...], preferred_element_type=F32)
    up = jnp.dot(n, wu_ref[...], preferred_element_type=F32)
    act = (0.5 * _silu(gate) * up).astype(BF16)
    o_ref[...] += jnp.dot(act, wd_ref[...], preferred_element_type=F32)

    if final_norm:
        @pl.when(j == pl.num_programs(1) - 1)
        def _():
            o_ref[...] = _rms(o_ref[...], fg_ref[...])


def _ffn(h, gain, wg, wu, wd, final_gain=None, tm=512, tf=512):
    s, d = h.shape
    dff = wg.shape[1]
    fg = jnp.ones((1, d), F32) if final_gain is None else final_gain.reshape(1, d)
    return pl.pallas_call(
        functools.partial(_ffn_kernel, final_norm=final_gain is not None),
        grid=(s // tm, dff // tf),
        in_specs=[
            pl.BlockSpec((tm, d), lambda i, j: (i, 0)),
            pl.BlockSpec((1, d), lambda i, j: (0, 0)),
            pl.BlockSpec((d, tf), lambda i, j: (0, j)),
            pl.BlockSpec((d, tf), lambda i, j: (0, j)),
            pl.BlockSpec((tf, d), lambda i, j: (j, 0)),
            pl.BlockSpec((1, d), lambda i, j: (0, 0)),
        ],
        out_specs=pl.BlockSpec((tm, d), lambda i, j: (i, 0)),
        out_shape=jax.ShapeDtypeStruct((s, d), F32),
        scratch_shapes=[pltpu.VMEM((tm, d), BF16)],
        compiler_params=_params("parallel", "arbitrary"),
        name="ffn",
    )(h, gain.reshape(1, d), wg, wu, wd, fg)


def _norm_proj_kernel(x_ref, g_ref, w_ref, we_ref, o_ref, e_ref, n_ref):
    j = pl.program_id(1)

    @pl.when(j == 0)
    def _():
        nb = _rms(x_ref[...], g_ref[...]).astype(BF16)
        n_ref[...] = nb
        e_ref[...] = jnp.dot(nb, we_ref[...], preferred_element_type=F32)

    acc = jnp.dot(n_ref[...], w_ref[...], preferred_element_type=F32)
    for c in range(o_ref.shape[0]):
        o_ref[c] = acc[:, c * LANE:(c + 1) * LANE].astype(o_ref.dtype)


def _norm_proj(h, gain, w_main, w_extra, tm=1024, tn=1024):
    s, d = h.shape
    n = w_main.shape[1]
    return pl.pallas_call(
        _norm_proj_kernel,
        grid=(s // tm, n // tn),
        in_specs=[
            pl.BlockSpec((tm, d), lambda i, j: (i, 0)),
            pl.BlockSpec((1, d), lambda i, j: (0, 0)),
            pl.BlockSpec((d, tn), lambda i, j: (0, j)),
            pl.BlockSpec((d, LANE), lambda i, j: (0, 0)),
        ],
        out_specs=[
            pl.BlockSpec((tn // LANE, tm, LANE), lambda i, j: (j, i, 0)),
            pl.BlockSpec((tm, LANE), lambda i, j: (i, 0)),
        ],
        out_shape=[
            jax.ShapeDtypeStruct((n // LANE, s, LANE), BF16),
            jax.ShapeDtypeStruct((s, LANE), F32),
        ],
        scratch_shapes=[pltpu.VMEM((tm, d), BF16)],
        compiler_params=_params("parallel", "arbitrary"),
        name="norm_proj",
    )(h, gain.reshape(1, d), w_main, w_extra)


def _proj_res_kernel(*refs, n_in):
    h_ref = refs[0]
    x_refs = refs[1:1 + n_in]
    w_refs = refs[1 + n_in:1 + 2 * n_in]
    o_ref = refs[1 + 2 * n_in]
    acc = h_ref[...]
    for x_ref, w_ref in zip(x_refs, w_refs):
        acc = acc + jnp.dot(x_ref[...], w_ref[...], preferred_element_type=F32)
    o_ref[...] = acc


def _proj_res(h, xs, ws, tm=1024, tn=1024):
    s, d = h.shape
    n_in = len(xs)
    in_specs = [pl.BlockSpec((tm, tn), lambda i, j: (i, j))]
    in_specs += [pl.BlockSpec((tm, x.shape[1]), lambda i, j: (i, 0)) for x in xs]
    in_specs += [pl.BlockSpec((w.shape[0], tn), lambda i, j: (0, j)) for w in ws]
    return pl.pallas_call(
        functools.partial(_proj_res_kernel, n_in=n_in),
        grid=(s // tm, d // tn),
        in_specs=in_specs,
        out_specs=pl.BlockSpec((tm, tn), lambda i, j: (i, j)),
        out_shape=jax.ShapeDtypeStruct((s, d), F32),
        compiler_params=_params("parallel", "arbitrary"),
        name="proj_res",
    )(h, *xs, *ws)


def _gla_kernel(q_ref, k_ref, v_ref, og_ref, lr_ref, w2_ref, b_ref, gn_ref, o_ref, st_ref, *, tb):
    t = pl.program_id(0)

    @pl.when(t == 0)
    def _():
        st_ref[...] = jnp.zeros_like(st_ref)

    tril = (_iota2((CHUNK, CHUNK), 0) >= _iota2((CHUNK, CHUNK), 1))
    tril_f = tril.astype(F32)
    gn = gn_ref[...]
    vs = GLA_DV // LANE
    z = _dot(lr_ref[...], w2_ref[...]) + b_ref[...]
    log_a = -_softplus(-z) / GLA_GATE_NORMALIZER
    chunks = range(tb // CHUNK)
    rows = [slice(c * CHUNK, (c + 1) * CHUNK) for c in chunks]
    units = [(h, c) for c in chunks for h in range(GLA_HEADS)]
    cum_all = [_dot_f32(tril_f, log_a[rows[c], :]) for c in chunks]
    intra, qe, upd, dec = {}, {}, {}, {}
    for u in units:
        h, c = u
        cum = cum_all[c][:, h * GLA_DK:(h + 1) * GLA_DK]
        mid = cum[CHUNK // 2 - 1:CHUNK // 2, :]
        last = cum[CHUNK - 1:CHUNK, :]
        q = q_ref[h, rows[c], :].astype(F32) * (GLA_DK ** -0.5)
        k = k_ref[h, rows[c], :].astype(F32)
        v = jnp.concatenate([v_ref[vs * h + i, rows[c], :] for i in range(vs)], axis=1)
        scores = _dot_nt(q * jnp.exp(cum - mid), k * jnp.exp(mid - cum))
        intra[u] = _dot(jnp.where(tril, scores, 0.0), v)
        qe[u] = q * jnp.exp(cum)
        upd[u] = _dot_tn(v, k * jnp.exp(last - cum))
        dec[u] = jnp.exp(last)
    for u in units:
        h, c = u
        st = st_ref[h]
        out = intra[u] + _dot_nt(qe[u], st)
        st_ref[h] = dec[u] * st + upd[u]
        og = jnp.concatenate([og_ref[vs * h + i, rows[c], :] for i in range(vs)], axis=1).astype(F32)
        o_ref[rows[c], h * GLA_DV:(h + 1) * GLA_DV] = (_rms(out, gn) * _silu(og)).astype(o_ref.dtype)


def _gla(slabs, extra, w2, bias, gnorm, tb=256):
    s = slabs.shape[1]
    nk = GLA_KEY // LANE
    return pl.pallas_call(
        functools.partial(_gla_kernel, tb=tb),
        grid=(s // tb,),
        in_specs=[
            pl.BlockSpec((nk, tb, LANE), lambda t: (0, t, 0)),
            pl.BlockSpec((nk, tb, LANE), lambda t: (1, t, 0)),
            pl.BlockSpec((2 * nk, tb, LANE), lambda t: (1, t, 0)),
            pl.BlockSpec((2 * nk, tb, LANE), lambda t: (2, t, 0)),
            pl.BlockSpec((tb, LANE), lambda t: (t, 0)),
            pl.BlockSpec((LANE, GLA_KEY), lambda t: (0, 0)),
            pl.BlockSpec((1, GLA_KEY), lambda t: (0, 0)),
            pl.BlockSpec((1, GLA_DV), lambda t: (0, 0)),
        ],
        out_specs=pl.BlockSpec((tb, GLA_VAL), lambda t: (t, 0)),
        out_shape=jax.ShapeDtypeStruct((s, GLA_VAL), BF16),
        scratch_shapes=[pltpu.VMEM((GLA_HEADS, GLA_DV, GLA_DK), F32)],
        compiler_params=_params("arbitrary"),
        name="gla",
    )(slabs, slabs, slabs, slabs, extra, w2, bias, gnorm)


def _conv_rows(hist_ref, w, r0, n):
    acc = None
    for kk in range(CONV_WIDTH):
        off = CONV_PAD + r0 - (CONV_WIDTH - 1) + kk
        term = w[kk:kk + 1, :] * hist_ref[off:off + n, :]
        acc = term if acc is None else acc + term
    return acc


def _conv_carry(hist_ref, tb):
    hist_ref[0:CONV_PAD, :] = hist_ref[tb:tb + CONV_PAD, :]


def _causal_conv(hist_ref, x, w, tb):
    hist_ref[CONV_PAD:CONV_PAD + tb, :] = x
    acc = _conv_rows(hist_ref, w, 0, tb)
    _conv_carry(hist_ref, tb)
    return acc


def _lru_kernel(x_ref, y_ref, cw_ref, cb_ref, wa_ref, ba_ref, wx_ref, bx_ref, lam_ref,
                o_ref, hist_ref, h_ref, *, tb):
    t = pl.program_id(1)

    @pl.when(t == 0)
    def _():
        hist_ref[0:CONV_PAD, :] = jnp.zeros((CONV_PAD, LANE), F32)
        h_ref[...] = jnp.zeros_like(h_ref)

    xc = _causal_conv(hist_ref, x_ref[0].astype(F32), cw_ref[...], tb) + cb_ref[...]
    r = jax.nn.sigmoid(_dot(xc, wa_ref[0]) + ba_ref[...])
    i = jax.nn.sigmoid(_dot(xc, wx_ref[0]) + bx_ref[...])
    log_a = -LRU_C * r * _softplus(-lam_ref[...])
    a = jnp.exp(log_a)
    u = jnp.sqrt(-jnp.tanh(log_a) * (a * a + 1.0)) * (i * xc)

    row = _iota2((tb, LANE), 0)
    sh = 1
    while sh < tb:
        keep = row >= sh
        a_prev = jnp.where(keep, pltpu.roll(a, sh, 0), 1.0)
        u_prev = jnp.where(keep, pltpu.roll(u, sh, 0), 0.0)
        u = u + a * u_prev
        a = a * a_prev
        sh *= 2
    h = u + a * h_ref[0:1, :]
    h_ref[...] = jnp.broadcast_to(h[tb - 1:tb, :], h_ref.shape)
    o_ref[...] = (h * jax.nn.gelu(y_ref[0].astype(F32))).astype(o_ref.dtype)


def _lru(slabs, x_slab0, y_slab0, conv_w, conv_b, w_a, b_a, w_x, b_x, lam, tb=512):
    s = slabs.shape[1]
    row = lambda n, t: (0, n)
    return pl.pallas_call(
        functools.partial(_lru_kernel, tb=tb),
        grid=(LRU_BLOCKS, s // tb),
        in_specs=[
            pl.BlockSpec((1, tb, LANE), lambda n, t: (x_slab0 + n, t, 0)),
            pl.BlockSpec((1, tb, LANE), lambda n, t: (y_slab0 + n, t, 0)),
            pl.BlockSpec((CONV_WIDTH, LANE), row),
            pl.BlockSpec((1, LANE), row),
            pl.BlockSpec((1, LANE, LANE), lambda n, t: (n, 0, 0)),
            pl.BlockSpec((1, LANE), row),
            pl.BlockSpec((1, LANE, LANE), lambda n, t: (n, 0, 0)),
            pl.BlockSpec((1, LANE), row),
            pl.BlockSpec((1, LANE), row),
        ],
        out_specs=pl.BlockSpec((tb, LANE), lambda n, t: (t, n)),
        out_shape=jax.ShapeDtypeStruct((s, LRU_WIDTH), BF16),
        scratch_shapes=[pltpu.VMEM((tb + CONV_PAD, LANE), F32), pltpu.VMEM((8, LANE), F32)],
        compiler_params=_params("parallel", "arbitrary"),
        name="rg_lru",
    )(slabs, slabs, conv_w, conv_b.reshape(1, -1), w_a, b_a.reshape(1, -1), w_x,
      b_x.reshape(1, -1), lam.reshape(1, -1))


def _gdn_gates_kernel(e_ref, alog_ref, dtb_ref, cols_ref, rows_ref, *, tb):
    ex = e_ref[...]
    is_beta = _iota2((CHUNK, LANE), 1) < GDN_V_HEADS
    beta = jax.nn.sigmoid(ex)
    g = -jnp.exp(alog_ref[...]) * _softplus(ex + dtb_ref[...])
    tril_f = (_iota2((CHUNK, CHUNK), 0) >= _iota2((CHUNK, CHUNK), 1)).astype(F32)
    for c in range(tb // CHUNK):
        rows = slice(c * CHUNK, (c + 1) * CHUNK)
        gcum = _dot_f32(tril_f, g[rows, :])
        cols_ref[rows, :] = jnp.where(is_beta, beta[rows, :], gcum)
        rows_ref[c] = gcum.T


def _gdn_gates(extra, a_log, dt_bias, tb=512):
    s = extra.shape[0]
    alog_row = jnp.zeros((1, LANE), F32).at[0, GDN_V_HEADS:2 * GDN_V_HEADS].set(a_log)
    dtb_row = jnp.zeros((1, LANE), F32).at[0, GDN_V_HEADS:2 * GDN_V_HEADS].set(dt_bias)
    return pl.pallas_call(
        functools.partial(_gdn_gates_kernel, tb=tb),
        grid=(s // tb,),
        in_specs=[
            pl.BlockSpec((tb, LANE), lambda t: (t, 0)),
            pl.BlockSpec((1, LANE), lambda t: (0, 0)),
            pl.BlockSpec((1, LANE), lambda t: (0, 0)),
        ],
        out_specs=[
            pl.BlockSpec((tb, LANE), lambda t: (t, 0)),
            pl.BlockSpec((tb // CHUNK, LANE, CHUNK), lambda t: (t, 0, 0)),
        ],
        out_shape=[
            jax.ShapeDtypeStruct((s, LANE), F32),
            jax.ShapeDtypeStruct((s // CHUNK, LANE, CHUNK), F32),
        ],
        compiler_params=_params("parallel"),
        name="gdn_gates",
    )(extra, alog_row, dtb_row)


def _gdn_kernel(q_ref, k_ref, v_ref, z_ref, cols_ref, rows_ref, cwq_ref, cwk_ref, cwv_ref,
                on_ref, o_ref, qh_ref, kh_ref, vh_ref, st_ref, *, tb):
    hg = pl.program_id(0)
    t = pl.program_id(1)
    heads = range(GDN_GROUP)

    hists = ((qh_ref, q_ref), (kh_ref, k_ref), (vh_ref, v_ref))

    @pl.when(t == 0)
    def _():
        for hist_ref, _ in hists:
            hist_ref[0:CONV_PAD_BF16, :] = jnp.zeros((CONV_PAD_BF16, hist_ref.shape[1]), BF16)
        st_ref[...] = jnp.zeros_like(st_ref)

    for hist_ref, x_ref in hists:
        for i in range(x_ref.shape[0]):
            hist_ref[CONV_PAD_BF16:CONV_PAD_BF16 + tb, i * LANE:(i + 1) * LANE] = x_ref[i]
    cwq, cwk, cwv = cwq_ref[...], cwk_ref[...], cwv_ref[...]

    ext = CONV_PAD_BF16 + CHUNK
    tap_row = _iota2((CONV_WIDTH * CHUNK, ext), 0)
    tap_col = _iota2((CONV_WIDTH * CHUNK, ext), 1)
    shift = (tap_col == tap_row % CHUNK + tap_row // CHUNK
             + (CONV_PAD_BF16 - (CONV_WIDTH - 1))).astype(BF16)

    def conv_silu(hist_ref, w, c):
        taps = jnp.dot(shift, hist_ref[c * CHUNK:c * CHUNK + ext, :], preferred_element_type=F32)
        acc = w[0:1, :] * taps[0:CHUNK]
        for kk in range(1, CONV_WIDTH):
            acc = acc + w[kk:kk + 1, :] * taps[kk * CHUNK:(kk + 1) * CHUNK]
        return _silu(acc)

    def l2n(x):
        return x * lax.rsqrt(jnp.sum(x * x, axis=-1, keepdims=True) + L2_EPS)

    lane = _iota2((CHUNK, LANE), 1)
    sub = _iota2((CHUNK, LANE), 0)
    left = lane < CHUNK
    col = jnp.where(left, lane, lane - CHUNK)
    incl = sub >= col
    strict = sub > col
    eye_p = (sub == col).astype(F32)
    on = on_ref[...]

    def block_diag(m):
        return jnp.concatenate([jnp.where(left, m, 0.0), jnp.where(left, 0.0, m)], axis=0)

    chunks = range(tb // CHUNK)
    units = [(j, c) for c in chunks for j in heads]
    vheads = range(GDN_REP)
    rows = [slice(c * CHUNK, (c + 1) * CHUNK) for c in chunks]
    q, k, v, beta, gcum, glast, xpw, tinv, attn = {}, {}, {}, {}, {}, {}, {}, {}, {}
    for u in units:
        j, c = u
        if j == 0:
            qc = conv_silu(qh_ref, cwq, c)
            kc = conv_silu(kh_ref, cwk, c)
            v[c] = conv_silu(vh_ref, cwv, c)
            for jj in heads:
                q[jj, c] = l2n(qc[:, jj * LANE:(jj + 1) * LANE]) * (GDN_DK ** -0.5)
                k[jj, c] = l2n(kc[:, jj * LANE:(jj + 1) * LANE])
        hq = hg * GDN_GROUP + j
        cols = cols_ref[rows[c], :]
        hv = [hq * GDN_REP + e for e in vheads]
        beta[u] = [jnp.sum(jnp.where(lane == h, cols, 0.0), axis=1, keepdims=True) for h in hv]
        gcum[u] = [jnp.sum(jnp.where(lane == h + GDN_V_HEADS, cols, 0.0), axis=1, keepdims=True)
                   for h in hv]
        grow = [rows_ref[c, pl.ds(GDN_V_HEADS + h, 1), :] for h in hv]
        grow_p = jnp.concatenate(grow, axis=1)
        glast[u] = [r[:, CHUNK - 1:CHUNK] for r in grow]
        gcol_p = jnp.where(left, gcum[u][0], gcum[u][1])
        decay = jnp.where(incl, jnp.exp(jnp.where(incl, gcol_p - grow_p, 0.0)), 0.0)
        kq = _dot_nt(jnp.concatenate([k[u], q[u]], axis=0), jnp.concatenate([k[u], k[u]], axis=0))
        beta_p = jnp.where(left, beta[u][0], beta[u][1])
        lmat = jnp.where(strict, kq[:CHUNK] * beta_p * decay, 0.0)
        attn[u] = kq[CHUNK:] * decay
        xpw[u] = -lmat
        tinv[u] = eye_p + xpw[u]
    for hist_ref, _ in hists:
        hist_ref[0:CONV_PAD_BF16, :] = hist_ref[tb:tb + CONV_PAD_BF16, :]
    for u in units:
        xpw[u] = _dot(xpw[u], block_diag(xpw[u]))
    for _ in range(4):
        for u in units:
            both = _dot(jnp.concatenate([tinv[u], xpw[u]], axis=0), block_diag(xpw[u]))
            tinv[u] = tinv[u] + both[:CHUNK]
            xpw[u] = both[CHUNK:]
    for u in units:
        tinv[u] = block_diag(tinv[u] + _dot(tinv[u], block_diag(xpw[u])))
    sol, att_sol = {}, {}
    for u in units:
        j, c = u
        rhs = jnp.concatenate(
            [jnp.concatenate([v[c][:, (j * GDN_REP + e) * LANE:(j * GDN_REP + e + 1) * LANE]
                              * beta[u][e],
                              k[u] * (beta[u][e] * jnp.exp(gcum[u][e]))], axis=1)
             for e in vheads], axis=0)
        sol[u] = _dot(tinv[u], rhs)
    for u in units:
        att_sol[u] = _dot(block_diag(attn[u]), sol[u])
    lhs, upd = {}, {}
    for u in units:
        for e in vheads:
            hrows = slice(e * CHUNK, (e + 1) * CHUNK)
            kd = k[u] * jnp.exp(glast[u][e] - gcum[u][e])
            kd_sol = _dot_tn(kd, sol[u][hrows, :])
            qd = q[u] * jnp.exp(gcum[u][e]) - att_sol[u][hrows, GDN_DV:]
            lhs[u, e] = jnp.concatenate([kd_sol[:, GDN_DV:], qd], axis=0)
            upd[u, e] = kd_sol[:, :GDN_DV]
    for u in units:
        j, c = u
        for e in vheads:
            hrows = slice(e * CHUNK, (e + 1) * CHUNK)
            hv = j * GDN_REP + e
            st = st_ref[hv]
            prod = _dot(lhs[u, e], st)
            st_ref[hv] = jnp.exp(glast[u][e]) * st + upd[u, e] - prod[:GDN_DK]
            out = prod[GDN_DK:] + att_sol[u][hrows, :GDN_DV]
            zg = z_ref[hv, rows[c], :].astype(F32)
            o_ref[rows[c], hv * LANE:(hv + 1) * LANE] = (_rms(out, on) * _silu(zg)).astype(o_ref.dtype)


def _gdn(slabs, cols, rows, conv_w, out_norm, tb=512):
    s = slabs.shape[1]
    g = GDN_GROUP
    gv = GDN_GROUP * GDN_REP
    nq = GDN_QK_HEADS // g
    return pl.pallas_call(
        functools.partial(_gdn_kernel, tb=tb),
        grid=(nq, s // tb),
        in_specs=[
            pl.BlockSpec((g, tb, LANE), lambda h, t: (h, t, 0)),
            pl.BlockSpec((g, tb, LANE), lambda h, t: (nq + h, t, 0)),
            pl.BlockSpec((gv, tb, LANE), lambda h, t: (nq + h, t, 0)),
            pl.BlockSpec((gv, tb, LANE), lambda h, t: (2 * nq + h, t, 0)),
            pl.BlockSpec((tb, LANE), lambda h, t: (t, 0)),
            pl.BlockSpec((tb // CHUNK, LANE, CHUNK), lambda h, t: (t, 0, 0)),
            pl.BlockSpec((CONV_WIDTH, g * LANE), lambda h, t: (0, h)),
            pl.BlockSpec((CONV_WIDTH, g * LANE), lambda h, t: (0, nq + h)),
            pl.BlockSpec((CONV_WIDTH, gv * LANE), lambda h, t: (0, nq + h)),
            pl.BlockSpec((1, GDN_DV), lambda h, t: (0, 0)),
        ],
        out_specs=pl.BlockSpec((tb, gv * LANE), lambda h, t: (t, h)),
        out_shape=jax.ShapeDtypeStruct((s, GDN_VAL), BF16),
        scratch_shapes=[
            pltpu.VMEM((tb + CONV_PAD_BF16, g * LANE), BF16),
            pltpu.VMEM((tb + CONV_PAD_BF16, g * LANE), BF16),
            pltpu.VMEM((tb + CONV_PAD_BF16, gv * LANE), BF16),
            pltpu.VMEM((gv, GDN_DK, GDN_DV), F32),
        ],
        compiler_params=_params("parallel", "arbitrary"),
        name="gdn",
    )(slabs, slabs, slabs, slabs, cols, rows, conv_w, conv_w, conv_w, out_norm.reshape(1, -1))


def _pad_cols(w, n):
    return jnp.pad(w, ((0, 0), (0, n - w.shape[1])))


def _mixer_a(h, gain, w_in, gate_w2, gate_b, gla_norm, conv_w, conv_b, w_a, b_a, w_x, b_x, lam, w_out):
    c_lr = 2 * GLA_KEY + 2 * GLA_VAL
    c_x = c_lr + GLA_GATE_RANK
    w_main = jnp.concatenate([w_in[:, :c_lr], w_in[:, c_x:]], axis=1).astype(BF16)
    w_extra = _pad_cols(w_in[:, c_lr:c_x], LANE).astype(BF16)
    slabs, extra = _norm_proj(h, gain, w_main, w_extra)
    w2 = jnp.pad(gate_w2, ((0, LANE - GLA_GATE_RANK), (0, 0)))
    o_gla = _gla(slabs, extra, w2, gate_b.reshape(1, -1), gla_norm.reshape(1, -1))
    x_slab0 = c_lr // LANE
    o_lru = _lru(slabs, x_slab0, x_slab0 + LRU_WIDTH // LANE, conv_w, conv_b, w_a, b_a, w_x, b_x, lam)
    w_out = w_out.astype(BF16)
    return _proj_res(h, [o_gla, o_lru], [w_out[:GLA_VAL], w_out[GLA_VAL:]])


def _mixer_c(h, gain, w_in, conv_w, a_log, dt_bias, out_norm, w_out):
    c_main = 2 * GDN_KEY + 2 * GDN_VAL
    w_main = w_in[:, :c_main].astype(BF16)
    w_extra = _pad_cols(w_in[:, c_main:], LANE).astype(BF16)
    slabs, extra = _norm_proj(h, gain, w_main, w_extra)
    cols, rows = _gdn_gates(extra, a_log, dt_bias)
    o = _gdn(slabs, cols, rows, conv_w, out_norm)
    return _proj_res(h, [o], [w_out.astype(BF16)])


def kernel(x, norms, ffn_w_gate, ffn_w_up, ffn_w_down, a_w_in, a_gla_gate_w2, a_gla_gate_b, a_gla_norm, a_rg_conv_w, a_rg_conv_b, a_rg_w_a, a_rg_b_a, a_rg_w_x, a_rg_b_x, a_rg_lambda, a_w_out, c_w_in, c_conv_w, c_a_log, c_dt_bias, c_out_norm, c_w_out, final_norm):
    b, s, d = x.shape
    depth = norms.shape[0]
    wg = ffn_w_gate.astype(BF16)
    wu = ffn_w_up.astype(BF16)
    wd = ffn_w_down.astype(BF16)
    outs = []
    for bi in range(b):
        h = x.reshape(s, d) if b == 1 else x[bi]
        for layer in range(depth):
            j = layer // 2
            h = _ffn(h, norms[layer, 0], wg[layer, 0], wu[layer, 0], wd[layer, 0])
            if layer % 2 == 0:
                h = _mixer_a(h, norms[layer, 1], a_w_in[j], a_gla_gate_w2[j], a_gla_gate_b[j],
                             a_gla_norm[j], a_rg_conv_w[j], a_rg_conv_b[j], a_rg_w_a[j], a_rg_b_a[j],
                             a_rg_w_x[j], a_rg_b_x[j], a_rg_lambda[j], a_w_out[j])
            else:
                h = _mixer_c(h, norms[layer, 1], c_w_in[j], c_conv_w[j], c_a_log[j], c_dt_bias[j],
                             c_out_norm[j], c_w_out[j])
            last = layer == depth - 1
            h = _ffn(h, norms[layer, 2], wg[layer, 1], wu[layer, 1], wd[layer, 1],
                     final_gain=final_norm if last else None)
        outs.append(h)
    return outs[0].reshape(b, s, d) if b == 1 else jnp.stack(outs, axis=0)
```

```python
import functools

import jax
import jax.numpy as jnp
from jax import lax
from jax.experimental import pallas as pl
from jax.experimental.pallas import tpu as pltpu

F32 = jnp.float32
BF16 = jnp.bfloat16
HIGHEST = lax.Precision.HIGHEST

LANE = 128
D_MODEL = 2048
D_FF = 5632
RMS_EPS = 1e-6
L2_EPS = 1e-6
CONV_WIDTH = 4
CONV_PAD = 8
CONV_PAD_BF16 = 16

GLA_HEADS = 4
GLA_DK = 128
GLA_DV = 256
GLA_KEY = GLA_HEADS * GLA_DK
GLA_VAL = GLA_HEADS * GLA_DV
GLA_GATE_RANK = 16
GLA_GATE_NORMALIZER = 16.0
CHUNK = 64

LRU_WIDTH = 1024
LRU_BLOCKS = 8
LRU_C = 8.0

GDN_QK_HEADS = 16
GDN_V_HEADS = 32
GDN_REP = GDN_V_HEADS // GDN_QK_HEADS
GDN_GROUP = 2
GDN_DK = 128
GDN_DV = 128
GDN_KEY = GDN_QK_HEADS * GDN_DK
GDN_VAL = GDN_V_HEADS * GDN_DV

VMEM_LIMIT = 56 * 1024 * 1024


def _params(*sem):
    return pltpu.CompilerParams(dimension_semantics=sem, vmem_limit_bytes=VMEM_LIMIT)


def _rms(x, gain):
    return x * lax.rsqrt(jnp.mean(x * x, axis=-1, keepdims=True) + RMS_EPS) * gain


def _softplus(z):
    return jnp.maximum(z, 0.0) + jnp.log1p(jnp.exp(-jnp.abs(z)))


def _silu(z):
    return z * jax.nn.sigmoid(z)


def _dot(a, b):
    return jnp.dot(a.astype(BF16), b.astype(BF16), preferred_element_type=F32)


def _dot_nt(a, b):
    return lax.dot_general(a.astype(BF16), b.astype(BF16), (((1,), (1,)), ((), ())),
                           preferred_element_type=F32)


def _dot_tn(a, b):
    return lax.dot_general(a.astype(BF16), b.astype(BF16), (((0,), (0,)), ((), ())),
                           preferred_element_type=F32)


def _dot_f32(a, b):
    return jnp.dot(a, b, precision=HIGHEST, preferred_element_type=F32)


def _iota2(shape, dim):
    return lax.broadcasted_iota(jnp.int32, shape, dim)


def _ffn_kernel(x_ref, g_ref, wg_ref, wu_ref, wd_ref, fg_ref, o_ref, n_ref, *, final_norm):
    j = pl.program_id(1)

    @pl.when(j == 0)
    def _():
        x = x_ref[...]
        n_ref[...] = _rms(x, g_ref[...]).astype(BF16)
        o_ref[...] = x

    n = n_ref[...]
    gate = jnp.dot(n, wg_ref[...], preferred_element_type=F32)
    up = jnp.dot(n, wu_ref[...], preferred_element_type=F32)
    act = (0.5 * _silu(gate) * up).astype(BF16)
    o_ref[...] += jnp.dot(act, wd_ref[...], preferred_element_type=F32)

    if final_norm:
        @pl.when(j == pl.num_programs(1) - 1)
        def _():
            o_ref[...] = _rms(o_ref[...], fg_ref[...])


def _ffn(h, gain, wg, wu, wd, layer, idx, final_gain=None, tm=512, tf=512):
    s, d = h.shape
    dff = wg.shape[-1]
    fg = jnp.ones((1, d), F32) if final_gain is None else final_gain.reshape(1, d)
    return pl.pallas_call(
        functools.partial(_ffn_kernel, final_norm=final_gain is not None),
        grid=(s // tm, dff // tf),
        in_specs=[
            pl.BlockSpec((tm, d), lambda i, j: (i, 0)),
            pl.BlockSpec((1, d), lambda i, j: (0, 0)),
            pl.BlockSpec((None, None, d, tf), lambda i, j: (layer, idx, 0, j)),
            pl.BlockSpec((None, None, d, tf), lambda i, j: (layer, idx, 0, j)),
            pl.BlockSpec((None, None, tf, d), lambda i, j: (layer, idx, j, 0)),
            pl.BlockSpec((1, d), lambda i, j: (0, 0)),
        ],
        out_specs=pl.BlockSpec((tm, d), lambda i, j: (i, 0)),
        out_shape=jax.ShapeDtypeStruct((s, d), F32),
        scratch_shapes=[pltpu.VMEM((tm, d), BF16)],
        compiler_params=_params("parallel", "arbitrary"),
        name="ffn",
    )(h, gain.reshape(1, d), wg, wu, wd, fg)


def _norm_proj_kernel(x_ref, g_ref, w_ref, we_ref, o_ref, e_ref, n_ref):
    j = pl.program_id(1)

    @pl.when(j == 0)
    def _():
        nb = _rms(x_ref[...], g_ref[...]).astype(BF16)
        n_ref[...] = nb
        e_ref[...] = jnp.dot(nb, we_ref[...], preferred_element_type=F32)

    acc = jnp.dot(n_ref[...], w_ref[...], preferred_element_type=F32)
    for c in range(o_ref.shape[0]):
        o_ref[c] = acc[:, c * LANE:(c + 1) * LANE].astype(o_ref.dtype)


def _norm_proj(h, gain, w_main, w_extra, n=None, tm=1024, tn=1024):
    s, d = h.shape
    n = w_main.shape[1] if n is None else n
    return pl.pallas_call(
        _norm_proj_kernel,
        grid=(s // tm, n // tn),
        in_specs=[
            pl.BlockSpec((tm, d), lambda i, j: (i, 0)),
            pl.BlockSpec((1, d), lambda i, j: (0, 0)),
            pl.BlockSpec((d, tn), lambda i, j: (0, j)),
            pl.BlockSpec((d, LANE), lambda i, j: (0, 0)),
        ],
        out_specs=[
            pl.BlockSpec((tn // LANE, tm, LANE), lambda i, j: (j, i, 0)),
            pl.BlockSpec((tm, LANE), lambda i, j: (i, 0)),
        ],
        out_shape=[
            jax.ShapeDtypeStruct((n // LANE, s, LANE), BF16),
            jax.ShapeDtypeStruct((s, LANE), F32),
        ],
        scratch_shapes=[pltpu.VMEM((tm, d), BF16)],
        compiler_params=_params("parallel", "arbitrary"),
        name="norm_proj",
    )(h, gain.reshape(1, d), w_main, w_extra)


def _proj_res_kernel(*refs, n_in):
    h_ref = refs[0]
    x_refs = refs[1:1 + n_in]
    w_refs = refs[1 + n_in:1 + 2 * n_in]
    o_ref = refs[1 + 2 * n_in]
    acc = h_ref[...]
    for x_ref, w_ref in zip(x_refs, w_refs):
        acc = acc + jnp.dot(x_ref[...], w_ref[...], preferred_element_type=F32)
    o_ref[...] = acc


def _proj_res(h, xs, ws, tm=1024, tn=1024):
    s, d = h.shape
    n_in = len(xs)
    in_specs = [pl.BlockSpec((tm, tn), lambda i, j: (i, j))]
    in_specs += [pl.BlockSpec((tm, x.shape[1]), lambda i, j: (i, 0)) for x in xs]
    in_specs += [pl.BlockSpec((w.shape[0], tn), lambda i, j: (0, j)) for w in ws]
    return pl.pallas_call(
        functools.partial(_proj_res_kernel, n_in=n_in),
        grid=(s // tm, d // tn),
        in_specs=in_specs,
        out_specs=pl.BlockSpec((tm, tn), lambda i, j: (i, j)),
        out_shape=jax.ShapeDtypeStruct((s, d), F32),
        compiler_params=_params("parallel", "arbitrary"),
        name="proj_res",
    )(h, *xs, *ws)


def _gla_kernel(q_ref, k_ref, v_ref, og_ref, lr_ref, w2_ref, b_ref, gn_ref, o_ref, st_ref, *, tb):
    t = pl.program_id(0)

    @pl.when(t == 0)
    def _():
        st_ref[...] = jnp.zeros_like(st_ref)

    tril = (_iota2((CHUNK, CHUNK), 0) >= _iota2((CHUNK, CHUNK), 1))
    tril_f = tril.astype(F32)
    gn = gn_ref[...]
    vs = GLA_DV // LANE
    z = _dot(lr_ref[...], w2_ref[...]) + b_ref[...]
    log_a = -_softplus(-z) / GLA_GATE_NORMALIZER
    chunks = range(tb // CHUNK)
    rows = [slice(c * CHUNK, (c + 1) * CHUNK) for c in chunks]
    units = [(h, c) for c in chunks for h in range(GLA_HEADS)]
    cum_all = [_dot_f32(tril_f, log_a[rows[c], :]) for c in chunks]
    intra, qe, upd, dec = {}, {}, {}, {}
    for u in units:
        h, c = u
        cum = cum_all[c][:, h * GLA_DK:(h + 1) * GLA_DK]
        mid = cum[CHUNK // 2 - 1:CHUNK // 2, :]
        last = cum[CHUNK - 1:CHUNK, :]
        q = q_ref[h, rows[c], :].astype(F32) * (GLA_DK ** -0.5)
        k = k_ref[h, rows[c], :].astype(F32)
        v = jnp.concatenate([v_ref[vs * h + i, rows[c], :] for i in range(vs)], axis=1)
        scores = _dot_nt(q * jnp.exp(cum - mid), k * jnp.exp(mid - cum))
        intra[u] = _dot(jnp.where(tril, scores, 0.0), v)
        qe[u] = q * jnp.exp(cum)
        upd[u] = _dot_tn(v, k * jnp.exp(last - cum))
        dec[u] = jnp.exp(last)
    for u in units:
        h, c = u
        st = st_ref[h]
        out = intra[u] + _dot_nt(qe[u], st)
        st_ref[h] = dec[u] * st + upd[u]
        og = jnp.concatenate([og_ref[vs * h + i, rows[c], :] for i in range(vs)], axis=1).astype(F32)
        o_ref[rows[c], h * GLA_DV:(h + 1) * GLA_DV] = (_rms(out, gn) * _silu(og)).astype(o_ref.dtype)


def _gla(slabs, extra, w2, bias, gnorm, tb=256):
    s = slabs.shape[1]
    nk = GLA_KEY // LANE
    return pl.pallas_call(
        functools.partial(_gla_kernel, tb=tb),
        grid=(s // tb,),
        in_specs=[
            pl.BlockSpec((nk, tb, LANE), lambda t: (0, t, 0)),
            pl.BlockSpec((nk, tb, LANE), lambda t: (1, t, 0)),
            pl.BlockSpec((2 * nk, tb, LANE), lambda t: (1, t, 0)),
            pl.BlockSpec((2 * nk, tb, LANE), lambda t: (2, t, 0)),
            pl.BlockSpec((tb, LANE), lambda t: (t, 0)),
            pl.BlockSpec((LANE, GLA_KEY), lambda t: (0, 0)),
            pl.BlockSpec((1, GLA_KEY), lambda t: (0, 0)),
            pl.BlockSpec((1, GLA_DV), lambda t: (0, 0)),
        ],
        out_specs=pl.BlockSpec((tb, GLA_VAL), lambda t: (t, 0)),
        out_shape=jax.ShapeDtypeStruct((s, GLA_VAL), BF16),
        scratch_shapes=[pltpu.VMEM((GLA_HEADS, GLA_DV, GLA_DK), F32)],
        compiler_params=_params("arbitrary"),
        name="gla",
    )(slabs, slabs, slabs, slabs, extra, w2, bias, gnorm)


def _conv_rows(hist_ref, w, r0, n):
    acc = None
    for kk in range(CONV_WIDTH):
        off = CONV_PAD + r0 - (CONV_WIDTH - 1) + kk
        term = w[kk:kk + 1, :] * hist_ref[off:off + n, :]
        acc = term if acc is None else acc + term
    return acc


def _conv_carry(hist_ref, tb):
    hist_ref[0:CONV_PAD, :] = hist_ref[tb:tb + CONV_PAD, :]


def _causal_conv(hist_ref, x, w, tb):
    hist_ref[CONV_PAD:CONV_PAD + tb, :] = x
    acc = _conv_rows(hist_ref, w, 0, tb)
    _conv_carry(hist_ref, tb)
    return acc


def _lru_kernel(x_ref, y_ref, cw_ref, cb_ref, wa_ref, ba_ref, wx_ref, bx_ref, lam_ref,
                o_ref, hist_ref, h_ref, *, tb):
    t = pl.program_id(1)

    @pl.when(t == 0)
    def _():
        hist_ref[0:CONV_PAD, :] = jnp.zeros((CONV_PAD, LANE), F32)
        h_ref[...] = jnp.zeros_like(h_ref)

    xc = _causal_conv(hist_ref, x_ref[0].astype(F32), cw_ref[...], tb) + cb_ref[...]
    r = jax.nn.sigmoid(_dot(xc, wa_ref[0]) + ba_ref[...])
    i = jax.nn.sigmoid(_dot(xc, wx_ref[0]) + bx_ref[...])
    log_a = -LRU_C * r * _softplus(-lam_ref[...])
    a = jnp.exp(log_a)
    u = jnp.sqrt(-jnp.tanh(log_a) * (a * a + 1.0)) * (i * xc)

    row = _iota2((tb, LANE), 0)
    sh = 1
    while sh < tb:
        keep = row >= sh
        a_prev = jnp.where(keep, pltpu.roll(a, sh, 0), 1.0)
        u_prev = jnp.where(keep, pltpu.roll(u, sh, 0), 0.0)
        u = u + a * u_prev
        a = a * a_prev
        sh *= 2
    h = u + a * h_ref[0:1, :]
    h_ref[...] = jnp.broadcast_to(h[tb - 1:tb, :], h_ref.shape)
    o_ref[...] = (h * jax.nn.gelu(y_ref[0].astype(F32))).astype(o_ref.dtype)


def _lru(slabs, x_slab0, y_slab0, conv_w, conv_b, w_a, b_a, w_x, b_x, lam, tb=512):
    s = slabs.shape[1]
    row = lambda n, t: (0, n)
    return pl.pallas_call(
        functools.partial(_lru_kernel, tb=tb),
        grid=(LRU_BLOCKS, s // tb),
        in_specs=[
            pl.BlockSpec((1, tb, LANE), lambda n, t: (x_slab0 + n, t, 0)),
            pl.BlockSpec((1, tb, LANE), lambda n, t: (y_slab0 + n, t, 0)),
            pl.BlockSpec((CONV_WIDTH, LANE), row),
            pl.BlockSpec((1, LANE), row),
            pl.BlockSpec((1, LANE, LANE), lambda n, t: (n, 0, 0)),
            pl.BlockSpec((1, LANE), row),
            pl.BlockSpec((1, LANE, LANE), lambda n, t: (n, 0, 0)),
            pl.BlockSpec((1, LANE), row),
            pl.BlockSpec((1, LANE), row),
        ],
        out_specs=pl.BlockSpec((tb, LANE), lambda n, t: (t, n)),
        out_shape=jax.ShapeDtypeStruct((s, LRU_WIDTH), BF16),
        scratch_shapes=[pltpu.VMEM((tb + CONV_PAD, LANE), F32), pltpu.VMEM((8, LANE), F32)],
        compiler_params=_params("parallel", "arbitrary"),
        name="rg_lru",
    )(slabs, slabs, conv_w, conv_b.reshape(1, -1), w_a, b_a.reshape(1, -1), w_x,
      b_x.reshape(1, -1), lam.reshape(1, -1))


def _gdn_gates_kernel(e_ref, alog_ref, dtb_ref, cols_ref, rows_ref, *, tb):
    ex = e_ref[...]
    is_beta = _iota2((CHUNK, LANE), 1) < GDN_V_HEADS
    beta = jax.nn.sigmoid(ex)
    g = -jnp.exp(alog_ref[...]) * _softplus(ex + dtb_ref[...])
    tril_f = (_iota2((CHUNK, CHUNK), 0) >= _iota2((CHUNK, CHUNK), 1)).astype(F32)
    for c in range(tb // CHUNK):
        rows = slice(c * CHUNK, (c + 1) * CHUNK)
        gcum = _dot_f32(tril_f, g[rows, :])
        cols_ref[rows, :] = jnp.where(is_beta, beta[rows, :], gcum)
        rows_ref[c] = gcum.T


def _gdn_gates(extra, a_log, dt_bias, tb=512):
    s = extra.shape[0]
    alog_row = jnp.zeros((1, LANE), F32).at[0, GDN_V_HEADS:2 * GDN_V_HEADS].set(a_log)
    dtb_row = jnp.zeros((1, LANE), F32).at[0, GDN_V_HEADS:2 * GDN_V_HEADS].set(dt_bias)
    return pl.pallas_call(
        functools.partial(_gdn_gates_kernel, tb=tb),
        grid=(s // tb,),
        in_specs=[
            pl.BlockSpec((tb, LANE), lambda t: (t, 0)),
            pl.BlockSpec((1, LANE), lambda t: (0, 0)),
            pl.BlockSpec((1, LANE), lambda t: (0, 0)),
        ],
        out_specs=[
            pl.BlockSpec((tb, LANE), lambda t: (t, 0)),
            pl.BlockSpec((tb // CHUNK, LANE, CHUNK), lambda t: (t, 0, 0)),
        ],
        out_shape=[
            jax.ShapeDtypeStruct((s, LANE), F32),
            jax.ShapeDtypeStruct((s // CHUNK, LANE, CHUNK), F32),
        ],
        compiler_params=_params("parallel"),
        name="gdn_gates",
    )(extra, alog_row, dtb_row)


def _gdn_kernel(q_ref, k_ref, v_ref, z_ref, cols_ref, rows_ref, cwq_ref, cwk_ref, cwv_ref,
                on_ref, o_ref, qh_ref, kh_ref, vh_ref, qt_ref, kt_ref, vt_ref, st_ref, *, tb):
    hg = pl.program_id(0)
    t = pl.program_id(1)
    heads = range(GDN_GROUP)

    hists = ((qh_ref, qt_ref, q_ref), (kh_ref, kt_ref, k_ref), (vh_ref, vt_ref, v_ref))

    @pl.when(t == 0)
    def _():
        for _, tail_ref, _ in hists:
            tail_ref[...] = jnp.zeros_like(tail_ref)
        st_ref[...] = jnp.zeros_like(st_ref)

    for hist_ref, tail_ref, x_ref in hists:
        hist_ref[0:CONV_PAD_BF16, :] = tail_ref[...]
        for i in range(x_ref.shape[0]):
            hist_ref[CONV_PAD_BF16:CONV_PAD_BF16 + tb, i * LANE:(i + 1) * LANE] = x_ref[i]
    cwq, cwk, cwv = cwq_ref[...], cwk_ref[...], cwv_ref[...]

    ext = CONV_PAD_BF16 + CHUNK
    tap_row = _iota2((CONV_WIDTH * CHUNK, ext), 0)
    tap_col = _iota2((CONV_WIDTH * CHUNK, ext), 1)
    shift = (tap_col == tap_row % CHUNK + tap_row // CHUNK
             + (CONV_PAD_BF16 - (CONV_WIDTH - 1))).astype(BF16)

    def conv_silu(hist_ref, w, c):
        taps = jnp.dot(shift, hist_ref[c * CHUNK:c * CHUNK + ext, :], preferred_element_type=F32)
        acc = w[0:1, :] * taps[0:CHUNK]
        for kk in range(1, CONV_WIDTH):
            acc = acc + w[kk:kk + 1, :] * taps[kk * CHUNK:(kk + 1) * CHUNK]
        return _silu(acc)

    def l2n(x):
        return x * lax.rsqrt(jnp.sum(x * x, axis=-1, keepdims=True) + L2_EPS)

    lane = _iota2((CHUNK, LANE), 1)
    sub = _iota2((CHUNK, LANE), 0)
    left = lane < CHUNK
    col = jnp.where(left, lane, lane - CHUNK)
    incl = sub >= col
    strict = sub > col
    eye_p = (sub == col).astype(F32)
    on = on_ref[...]

    def block_diag(m):
        return jnp.concatenate([jnp.where(left, m, 0.0), jnp.where(left, 0.0, m)], axis=0)

    chunks = range(tb // CHUNK)
    units = [(j, c) for c in chunks for j in heads]
    vheads = range(GDN_REP)
    rows = [slice(c * CHUNK, (c + 1) * CHUNK) for c in chunks]
    q, k, v, beta, gcum, glast, decay, kq, xpw, tinv, attn = ({} for _ in range(11))
    qc = {c: conv_silu(qh_ref, cwq, c) for c in chunks}
    kc = {c: conv_silu(kh_ref, cwk, c) for c in chunks}
    v = {c: conv_silu(vh_ref, cwv, c) for c in chunks}
    for u in units:
        j, c = u
        q[u] = l2n(qc[c][:, j * LANE:(j + 1) * LANE]) * (GDN_DK ** -0.5)
        k[u] = l2n(kc[c][:, j * LANE:(j + 1) * LANE])
    for u in units:
        j, c = u
        cols = cols_ref[rows[c], :]
        hv = [(hg * GDN_GROUP + j) * GDN_REP + e for e in vheads]
        beta[u] = [jnp.sum(jnp.where(lane == h, cols, 0.0), axis=1, keepdims=True) for h in hv]
        gcum[u] = [jnp.sum(jnp.where(lane == h + GDN_V_HEADS, cols, 0.0), axis=1, keepdims=True)
                   for h in hv]
        grow = [rows_ref[c, pl.ds(GDN_V_HEADS + h, 1), :] for h in hv]
        grow_p = jnp.concatenate(grow, axis=1)
        glast[u] = [r[:, CHUNK - 1:CHUNK] for r in grow]
        gcol_p = jnp.where(left, gcum[u][0], gcum[u][1])
        decay[u] = jnp.where(incl, jnp.exp(jnp.where(incl, gcol_p - grow_p, 0.0)), 0.0)
    for u in units:
        kq[u] = _dot_nt(jnp.concatenate([k[u], q[u]], axis=0), jnp.concatenate([k[u], k[u]], axis=0))
    for u in units:
        beta_p = jnp.where(left, beta[u][0], beta[u][1])
        lmat = jnp.where(strict, kq[u][:CHUNK] * beta_p * decay[u], 0.0)
        attn[u] = kq[u][CHUNK:] * decay[u]
        xpw[u] = -lmat
        tinv[u] = eye_p + xpw[u]
    for u in units:
        xpw[u] = _dot(xpw[u], block_diag(xpw[u]))
    for _ in range(4):
        for u in units:
            both = _dot(jnp.concatenate([tinv[u], xpw[u]], axis=0), block_diag(xpw[u]))
            tinv[u] = tinv[u] + both[:CHUNK]
            xpw[u] = both[CHUNK:]
    for u in units:
        tinv[u] = block_diag(tinv[u] + _dot(tinv[u], block_diag(xpw[u])))
    sol, att_sol = {}, {}
    for u in units:
        j, c = u
        rhs = jnp.concatenate(
            [jnp.concatenate([v[c][:, (j * GDN_REP + e) * LANE:(j * GDN_REP + e + 1) * LANE]
                              * beta[u][e],
                              k[u] * (beta[u][e] * jnp.exp(gcum[u][e]))], axis=1)
             for e in vheads], axis=0)
        sol[u] = _dot(tinv[u], rhs)
    for u in units:
        att_sol[u] = _dot(block_diag(attn[u]), sol[u])
    lhs, upd = {}, {}
    for u in units:
        for e in vheads:
            hrows = slice(e * CHUNK, (e + 1) * CHUNK)
            kd = k[u] * jnp.exp(glast[u][e] - gcum[u][e])
            kd_sol = _dot_tn(kd, sol[u][hrows, :])
            qd = q[u] * jnp.exp(gcum[u][e]) - att_sol[u][hrows, GDN_DV:]
            lhs[u, e] = jnp.concatenate([kd_sol[:, GDN_DV:], qd], axis=0)
            upd[u, e] = kd_sol[:, :GDN_DV]
    for u in units:
        j, c = u
        for e in vheads:
            hrows = slice(e * CHUNK, (e + 1) * CHUNK)
            hv = j * GDN_REP + e
            st = st_ref[hv]
            prod = _dot(lhs[u, e], st)
            st_ref[hv] = jnp.exp(glast[u][e]) * st + upd[u, e] - prod[:GDN_DK]
            out = prod[GDN_DK:] + att_sol[u][hrows, :GDN_DV]
            zg = z_ref[hv, rows[c], :].astype(F32)
            o_ref[rows[c], hv * LANE:(hv + 1) * LANE] = (_rms(out, on) * _silu(zg)).astype(o_ref.dtype)
    for _, tail_ref, x_ref in hists:
        for i in range(x_ref.shape[0]):
            tail_ref[:, i * LANE:(i + 1) * LANE] = x_ref[i, tb - CONV_PAD_BF16:tb, :]


def _gdn(slabs, cols, rows, conv_w, out_norm, tb=512):
    s = slabs.shape[1]
    g = GDN_GROUP
    gv = GDN_GROUP * GDN_REP
    nq = GDN_QK_HEADS // g
    return pl.pallas_call(
        functools.partial(_gdn_kernel, tb=tb),
        grid=(nq, s // tb),
        in_specs=[
            pl.BlockSpec((g, tb, LANE), lambda h, t: (h, t, 0)),
            pl.BlockSpec((g, tb, LANE), lambda h, t: (nq + h, t, 0)),
            pl.BlockSpec((gv, tb, LANE), lambda h, t: (nq + h, t, 0)),
            pl.BlockSpec((gv, tb, LANE), lambda h, t: (2 * nq + h, t, 0)),
            pl.BlockSpec((tb, LANE), lambda h, t: (t, 0)),
            pl.BlockSpec((tb // CHUNK, LANE, CHUNK), lambda h, t: (t, 0, 0)),
            pl.BlockSpec((CONV_WIDTH, g * LANE), lambda h, t: (0, h)),
            pl.BlockSpec((CONV_WIDTH, g * LANE), lambda h, t: (0, nq + h)),
            pl.BlockSpec((CONV_WIDTH, gv * LANE), lambda h, t: (0, nq + h)),
            pl.BlockSpec((1, GDN_DV), lambda h, t: (0, 0)),
        ],
        out_specs=pl.BlockSpec((tb, gv * LANE), lambda h, t: (t, h)),
        out_shape=jax.ShapeDtypeStruct((s, GDN_VAL), BF16),
        scratch_shapes=[
            pltpu.VMEM((tb + CONV_PAD_BF16, g * LANE), BF16),
            pltpu.VMEM((tb + CONV_PAD_BF16, g * LANE), BF16),
            pltpu.VMEM((tb + CONV_PAD_BF16, gv * LANE), BF16),
            pltpu.VMEM((CONV_PAD_BF16, g * LANE), BF16),
            pltpu.VMEM((CONV_PAD_BF16, g * LANE), BF16),
            pltpu.VMEM((CONV_PAD_BF16, gv * LANE), BF16),
            pltpu.VMEM((gv, GDN_DK, GDN_DV), F32),
        ],
        compiler_params=_params("parallel", "arbitrary"),
        name="gdn",
    )(slabs, slabs, slabs, slabs, cols, rows, conv_w, conv_w, conv_w, out_norm.reshape(1, -1))


def _pad_cols(w, n):
    return jnp.pad(w, ((0, 0), (0, n - w.shape[1])))


def _mixer_a(h, gain, w_in, gate_w2, gate_b, gla_norm, conv_w, conv_b, w_a, b_a, w_x, b_x, lam, w_out):
    c_lr = 2 * GLA_KEY + 2 * GLA_VAL
    c_x = c_lr + GLA_GATE_RANK
    w_main = jnp.concatenate([w_in[:, :c_lr], w_in[:, c_x:]], axis=1).astype(BF16)
    w_extra = _pad_cols(w_in[:, c_lr:c_x], LANE).astype(BF16)
    slabs, extra = _norm_proj(h, gain, w_main, w_extra)
    w2 = jnp.pad(gate_w2, ((0, LANE - GLA_GATE_RANK), (0, 0)))
    o_gla = _gla(slabs, extra, w2, gate_b.reshape(1, -1), gla_norm.reshape(1, -1))
    x_slab0 = c_lr // LANE
    o_lru = _lru(slabs, x_slab0, x_slab0 + LRU_WIDTH // LANE, conv_w, conv_b, w_a, b_a, w_x, b_x, lam)
    w_out = w_out.astype(BF16)
    return _proj_res(h, [o_gla, o_lru], [w_out[:GLA_VAL], w_out[GLA_VAL:]])


def _mixer_c(h, gain, w_in, conv_w, a_log, dt_bias, out_norm, w_out):
    c_main = 2 * GDN_KEY + 2 * GDN_VAL
    w_extra = _pad_cols(w_in[:, c_main:], LANE).astype(BF16)
    slabs, extra = _norm_proj(h, gain, w_in.astype(BF16), w_extra, n=c_main)
    cols, rows = _gdn_gates(extra, a_log, dt_bias)
    o = _gdn(slabs, cols, rows, conv_w, out_norm)
    return _proj_res(h, [o], [w_out.astype(BF16)])


def kernel(x, norms, ffn_w_gate, ffn_w_up, ffn_w_down, a_w_in, a_gla_gate_w2, a_gla_gate_b, a_gla_norm, a_rg_conv_w, a_rg_conv_b, a_rg_w_a, a_rg_b_a, a_rg_w_x, a_rg_b_x, a_rg_lambda, a_w_out, c_w_in, c_conv_w, c_a_log, c_dt_bias, c_out_norm, c_w_out, final_norm):
    b, s, d = x.shape
    depth = norms.shape[0]
    wg = ffn_w_gate.astype(BF16)
    wu = ffn_w_up.astype(BF16)
    wd = ffn_w_down.astype(BF16)
    outs = []
    for bi in range(b):
        h = x.reshape(s, d) if b == 1 else x[bi]
        for layer in range(depth):
            j = layer // 2
            h = _ffn(h, norms[layer, 0], wg, wu, wd, layer, 0)
            if layer % 2 == 0:
                h = _mixer_a(h, norms[layer, 1], a_w_in[j], a_gla_gate_w2[j], a_gla_gate_b[j],
                             a_gla_norm[j], a_rg_conv_w[j], a_rg_conv_b[j], a_rg_w_a[j], a_rg_b_a[j],
                             a_rg_w_x[j], a_rg_b_x[j], a_rg_lambda[j], a_w_out[j])
            else:
                h = _mixer_c(h, norms[layer, 1], c_w_in[j], c_conv_w[j], c_a_log[j], c_dt_bias[j],
                             c_out_norm[j], c_w_out[j])
            last = layer == depth - 1
            h = _ffn(h, norms[layer, 2], wg, wu, wd, layer, 1,
                     final_gain=final_norm if last else None)
        outs.append(h)
    return outs[0].reshape(b, s, d) if b == 1 else jnp.stack(outs, axis=0)
```

```python
import functools

import jax
import jax.numpy as jnp
from jax import lax
from jax.experimental import pallas as pl
from jax.experimental.pallas import tpu as pltpu

F32 = jnp.float32
BF16 = jnp.bfloat16
HIGHEST = lax.Precision.HIGHEST

LANE = 128
D_MODEL = 2048
D_FF = 5632
RMS_EPS = 1e-6
L2_EPS = 1e-6
CONV_WIDTH = 4
CONV_PAD = 8
CONV_PAD_BF16 = 16

GLA_HEADS = 4
GLA_DK = 128
GLA_DV = 256
GLA_KEY = GLA_HEADS * GLA_DK
GLA_VAL = GLA_HEADS * GLA_DV
GLA_GATE_RANK = 16
GLA_GATE_NORMALIZER = 16.0
CHUNK = 64

LRU_WIDTH = 1024
LRU_BLOCKS = 8
LRU_C = 8.0

GDN_QK_HEADS = 16
GDN_V_HEADS = 32
GDN_REP = GDN_V_HEADS // GDN_QK_HEADS
GDN_GROUP = 4
GDN_DK = 128
GDN_DV = 128
GDN_KEY = GDN_QK_HEADS * GDN_DK
GDN_VAL = GDN_V_HEADS * GDN_DV

VMEM_LIMIT = 56 * 1024 * 1024


def _params(*sem):
    return pltpu.CompilerParams(dimension_semantics=sem, vmem_limit_bytes=VMEM_LIMIT)


def _rms(x, gain):
    return x * lax.rsqrt(jnp.mean(x * x, axis=-1, keepdims=True) + RMS_EPS) * gain


def _softplus(z):
    return jnp.maximum(z, 0.0) + jnp.log1p(jnp.exp(-jnp.abs(z)))


def _silu(z):
    return z * jax.nn.sigmoid(z)


def _dot(a, b):
    return jnp.dot(a.astype(BF16), b.astype(BF16), preferred_element_type=F32)


def _dot_nt(a, b):
    return lax.dot_general(a.astype(BF16), b.astype(BF16), (((1,), (1,)), ((), ())),
                           preferred_element_type=F32)


def _dot_tn(a, b):
    return lax.dot_general(a.astype(BF16), b.astype(BF16), (((0,), (0,)), ((), ())),
                           preferred_element_type=F32)


def _dot_f32(a, b):
    return jnp.dot(a, b, precision=HIGHEST, preferred_element_type=F32)


def _iota2(shape, dim):
    return lax.broadcasted_iota(jnp.int32, shape, dim)


def _ffn_kernel(x_ref, g_ref, wg_ref, wu_ref, wd_ref, fg_ref, o_ref, n_ref, *, final_norm):
    j = pl.program_id(1)

    @pl.when(j == 0)
    def _():
        x = x_ref[...]
        n_ref[...] = _rms(x, g_ref[...]).astype(BF16)
        o_ref[...] = x

    n = n_ref[...]
    gate = jnp.dot(n, wg_ref[...].astype(BF16), preferred_element_type=F32)
    up = jnp.dot(n, wu_ref[...].astype(BF16), preferred_element_type=F32)
    act = (0.5 * _silu(gate) * up).astype(BF16)
    o_ref[...] += jnp.dot(act, wd_ref[...].astype(BF16), preferred_element_type=F32)

    if final_norm:
        @pl.when(j == pl.num_programs(1) - 1)
        def _():
            o_ref[...] = _rms(o_ref[...], fg_ref[...])


def _ffn(h, gain, wg, wu, wd, layer, idx, final_gain=None, tm=1024, tf=256):
    s, d = h.shape
    dff = wg.shape[-1]
    fg = jnp.ones((1, d), F32) if final_gain is None else final_gain.reshape(1, d)
    return pl.pallas_call(
        functools.partial(_ffn_kernel, final_norm=final_gain is not None),
        grid=(s // tm, dff // tf),
        in_specs=[
            pl.BlockSpec((tm, d), lambda i, j: (i, 0)),
            pl.BlockSpec((1, d), lambda i, j: (0, 0)),
            pl.BlockSpec((None, None, d, tf), lambda i, j: (layer, idx, 0, j)),
            pl.BlockSpec((None, None, d, tf), lambda i, j: (layer, idx, 0, j)),
            pl.BlockSpec((None, None, tf, d), lambda i, j: (layer, idx, j, 0)),
            pl.BlockSpec((1, d), lambda i, j: (0, 0)),
        ],
        out_specs=pl.BlockSpec((tm, d), lambda i, j: (i, 0)),
        out_shape=jax.ShapeDtypeStruct((s, d), F32),
        scratch_shapes=[pltpu.VMEM((tm, d), BF16)],
        compiler_params=_params("parallel", "arbitrary"),
        name="ffn",
    )(h, gain.reshape(1, d), wg, wu, wd, fg)


def _norm_proj_kernel(x_ref, g_ref, w_ref, we_ref, o_ref, e_ref, n_ref):
    j = pl.program_id(1)

    @pl.when(j == 0)
    def _():
        nb = _rms(x_ref[...], g_ref[...]).astype(BF16)
        n_ref[...] = nb
        e_ref[...] = jnp.dot(nb, we_ref[...], preferred_element_type=F32)

    acc = jnp.dot(n_ref[...], w_ref[...].astype(BF16), preferred_element_type=F32)
    for c in range(o_ref.shape[0]):
        o_ref[c] = acc[:, c * LANE:(c + 1) * LANE].astype(o_ref.dtype)


def _norm_proj(h, gain, w_main, w_extra, n=None, tm=1024, tn=1024):
    s, d = h.shape
    n = w_main.shape[1] if n is None else n
    return pl.pallas_call(
        _norm_proj_kernel,
        grid=(s // tm, n // tn),
        in_specs=[
            pl.BlockSpec((tm, d), lambda i, j: (i, 0)),
            pl.BlockSpec((1, d), lambda i, j: (0, 0)),
            pl.BlockSpec((d, tn), lambda i, j: (0, j)),
            pl.BlockSpec((d, LANE), lambda i, j: (0, 0)),
        ],
        out_specs=[
            pl.BlockSpec((tn // LANE, tm, LANE), lambda i, j: (j, i, 0)),
            pl.BlockSpec((tm, LANE), lambda i, j: (i, 0)),
        ],
        out_shape=[
            jax.ShapeDtypeStruct((n // LANE, s, LANE), BF16),
            jax.ShapeDtypeStruct((s, LANE), F32),
        ],
        scratch_shapes=[pltpu.VMEM((tm, d), BF16)],
        compiler_params=_params("parallel", "arbitrary"),
        name="norm_proj",
    )(h, gain.reshape(1, d), w_main, w_extra)


def _proj_res_kernel(*refs, n_in):
    h_ref = refs[0]
    x_refs = refs[1:1 + n_in]
    w_refs = refs[1 + n_in:1 + 2 * n_in]
    o_ref = refs[1 + 2 * n_in]
    acc = h_ref[...]
    for x_ref, w_ref in zip(x_refs, w_refs):
        acc = acc + jnp.dot(x_ref[...], w_ref[...], preferred_element_type=F32)
    o_ref[...] = acc


def _proj_res(h, xs, ws, tm=1024, tn=1024):
    s, d = h.shape
    n_in = len(xs)
    in_specs = [pl.BlockSpec((tm, tn), lambda i, j: (i, j))]
    in_specs += [pl.BlockSpec((tm, x.shape[1]), lambda i, j: (i, 0)) for x in xs]
    in_specs += [pl.BlockSpec((w.shape[0], tn), lambda i, j: (0, j)) for w in ws]
    return pl.pallas_call(
        functools.partial(_proj_res_kernel, n_in=n_in),
        grid=(s // tm, d // tn),
        in_specs=in_specs,
        out_specs=pl.BlockSpec((tm, tn), lambda i, j: (i, j)),
        out_shape=jax.ShapeDtypeStruct((s, d), F32),
        compiler_params=_params("parallel", "arbitrary"),
        name="proj_res",
    )(h, *xs, *ws)


def _gla_kernel(q_ref, k_ref, v_ref, og_ref, lr_ref, w2_ref, b_ref, gn_ref, o_ref, st_ref, *, tb):
    t = pl.program_id(0)

    @pl.when(t == 0)
    def _():
        st_ref[...] = jnp.zeros_like(st_ref)

    tril = (_iota2((CHUNK, CHUNK), 0) >= _iota2((CHUNK, CHUNK), 1))
    tril_f = tril.astype(F32)
    gn = gn_ref[...]
    vs = GLA_DV // LANE
    z = _dot(lr_ref[...], w2_ref[...]) + b_ref[...]
    log_a = -_softplus(-z) / GLA_GATE_NORMALIZER
    chunks = range(tb // CHUNK)
    rows = [slice(c * CHUNK, (c + 1) * CHUNK) for c in chunks]
    units = [(h, c) for c in chunks for h in range(GLA_HEADS)]
    cum_all = [_dot_f32(tril_f, log_a[rows[c], :]) for c in chunks]
    intra, qe, upd, dec = {}, {}, {}, {}
    for u in units:
        h, c = u
        cum = cum_all[c][:, h * GLA_DK:(h + 1) * GLA_DK]
        mid = cum[CHUNK // 2 - 1:CHUNK // 2, :]
        last = cum[CHUNK - 1:CHUNK, :]
        q = q_ref[h, rows[c], :].astype(F32) * (GLA_DK ** -0.5)
        k = k_ref[h, rows[c], :].astype(F32)
        v = jnp.concatenate([v_ref[vs * h + i, rows[c], :] for i in range(vs)], axis=1)
        scores = _dot_nt(q * jnp.exp(cum - mid), k * jnp.exp(mid - cum))
        intra[u] = _dot(jnp.where(tril, scores, 0.0), v)
        qe[u] = q * jnp.exp(cum)
        upd[u] = _dot_tn(v, k * jnp.exp(last - cum))
        dec[u] = jnp.exp(last)
    for u in units:
        h, c = u
        st = st_ref[h]
        out = intra[u] + _dot_nt(qe[u], st)
        st_ref[h] = dec[u] * st + upd[u]
        og = jnp.concatenate([og_ref[vs * h + i, rows[c], :] for i in range(vs)], axis=1).astype(F32)
        o_ref[rows[c], h * GLA_DV:(h + 1) * GLA_DV] = (_rms(out, gn) * _silu(og)).astype(o_ref.dtype)


def _gla(slabs, extra, w2, bias, gnorm, tb=256):
    s = slabs.shape[1]
    nk = GLA_KEY // LANE
    return pl.pallas_call(
        functools.partial(_gla_kernel, tb=tb),
        grid=(s // tb,),
        in_specs=[
            pl.BlockSpec((nk, tb, LANE), lambda t: (0, t, 0)),
            pl.BlockSpec((nk, tb, LANE), lambda t: (1, t, 0)),
            pl.BlockSpec((2 * nk, tb, LANE), lambda t: (1, t, 0)),
            pl.BlockSpec((2 * nk, tb, LANE), lambda t: (2, t, 0)),
            pl.BlockSpec((tb, LANE), lambda t: (t, 0)),
            pl.BlockSpec((LANE, GLA_KEY), lambda t: (0, 0)),
            pl.BlockSpec((1, GLA_KEY), lambda t: (0, 0)),
            pl.BlockSpec((1, GLA_DV), lambda t: (0, 0)),
        ],
        out_specs=pl.BlockSpec((tb, GLA_VAL), lambda t: (t, 0)),
        out_shape=jax.ShapeDtypeStruct((s, GLA_VAL), BF16),
        scratch_shapes=[pltpu.VMEM((GLA_HEADS, GLA_DV, GLA_DK), F32)],
        compiler_params=_params("arbitrary"),
        name="gla",
    )(slabs, slabs, slabs, slabs, extra, w2, bias, gnorm)


def _conv_rows(hist_ref, w, r0, n):
    acc = None
    for kk in range(CONV_WIDTH):
        off = CONV_PAD + r0 - (CONV_WIDTH - 1) + kk
        term = w[kk:kk + 1, :] * hist_ref[off:off + n, :]
        acc = term if acc is None else acc + term
    return acc


def _conv_carry(hist_ref, tb):
    hist_ref[0:CONV_PAD, :] = hist_ref[tb:tb + CONV_PAD, :]


def _causal_conv(hist_ref, x, w, tb):
    hist_ref[CONV_PAD:CONV_PAD + tb, :] = x
    acc = _conv_rows(hist_ref, w, 0, tb)
    _conv_carry(hist_ref, tb)
    return acc


def _lru_kernel(x_ref, y_ref, cw_ref, cb_ref, wa_ref, ba_ref, wx_ref, bx_ref, lam_ref,
                o_ref, hist_ref, h_ref, *, tb):
    t = pl.program_id(1)

    @pl.when(t == 0)
    def _():
        hist_ref[0:CONV_PAD, :] = jnp.zeros((CONV_PAD, LANE), F32)
        h_ref[...] = jnp.zeros_like(h_ref)

    xc = _causal_conv(hist_ref, x_ref[0].astype(F32), cw_ref[...], tb) + cb_ref[...]
    r = jax.nn.sigmoid(_dot(xc, wa_ref[0]) + ba_ref[...])
    i = jax.nn.sigmoid(_dot(xc, wx_ref[0]) + bx_ref[...])
    log_a = -LRU_C * r * _softplus(-lam_ref[...])
    a = jnp.exp(log_a)
    u = jnp.sqrt(-jnp.tanh(log_a) * (a * a + 1.0)) * (i * xc)

    row = _iota2((tb, LANE), 0)
    sh = 1
    while sh < tb:
        keep = row >= sh
        a_prev = jnp.where(keep, pltpu.roll(a, sh, 0), 1.0)
        u_prev = jnp.where(keep, pltpu.roll(u, sh, 0), 0.0)
        u = u + a * u_prev
        a = a * a_prev
        sh *= 2
    h = u + a * h_ref[0:1, :]
    h_ref[...] = jnp.broadcast_to(h[tb - 1:tb, :], h_ref.shape)
    o_ref[...] = (h * jax.nn.gelu(y_ref[0].astype(F32))).astype(o_ref.dtype)


def _lru(slabs, x_slab0, y_slab0, conv_w, conv_b, w_a, b_a, w_x, b_x, lam, tb=512):
    s = slabs.shape[1]
    row = lambda n, t: (0, n)
    return pl.pallas_call(
        functools.partial(_lru_kernel, tb=tb),
        grid=(LRU_BLOCKS, s // tb),
        in_specs=[
            pl.BlockSpec((1, tb, LANE), lambda n, t: (x_slab0 + n, t, 0)),
            pl.BlockSpec((1, tb, LANE), lambda n, t: (y_slab0 + n, t, 0)),
            pl.BlockSpec((CONV_WIDTH, LANE), row),
            pl.BlockSpec((1, LANE), row),
            pl.BlockSpec((1, LANE, LANE), lambda n, t: (n, 0, 0)),
            pl.BlockSpec((1, LANE), row),
            pl.BlockSpec((1, LANE, LANE), lambda n, t: (n, 0, 0)),
            pl.BlockSpec((1, LANE), row),
            pl.BlockSpec((1, LANE), row),
        ],
        out_specs=pl.BlockSpec((tb, LANE), lambda n, t: (t, n)),
        out_shape=jax.ShapeDtypeStruct((s, LRU_WIDTH), BF16),
        scratch_shapes=[pltpu.VMEM((tb + CONV_PAD, LANE), F32), pltpu.VMEM((8, LANE), F32)],
        compiler_params=_params("parallel", "arbitrary"),
        name="rg_lru",
    )(slabs, slabs, conv_w, conv_b.reshape(1, -1), w_a, b_a.reshape(1, -1), w_x,
      b_x.reshape(1, -1), lam.reshape(1, -1))


def _gdn_gates_kernel(e_ref, alog_ref, dtb_ref, cols_ref, rows_ref, *, tb):
    ex = e_ref[...]
    is_beta = _iota2((CHUNK, LANE), 1) < GDN_V_HEADS
    beta = jax.nn.sigmoid(ex)
    g = -jnp.exp(alog_ref[...]) * _softplus(ex + dtb_ref[...])
    tril_f = (_iota2((CHUNK, CHUNK), 0) >= _iota2((CHUNK, CHUNK), 1)).astype(F32)
    for c in range(tb // CHUNK):
        rows = slice(c * CHUNK, (c + 1) * CHUNK)
        gcum = _dot_f32(tril_f, g[rows, :])
        cols_ref[rows, :] = jnp.where(is_beta, beta[rows, :], gcum)
        rows_ref[c] = gcum.T


def _gdn_gates(extra, a_log, dt_bias, tb=512):
    s = extra.shape[0]
    alog_row = jnp.zeros((1, LANE), F32).at[0, GDN_V_HEADS:2 * GDN_V_HEADS].set(a_log)
    dtb_row = jnp.zeros((1, LANE), F32).at[0, GDN_V_HEADS:2 * GDN_V_HEADS].set(dt_bias)
    return pl.pallas_call(
        functools.partial(_gdn_gates_kernel, tb=tb),
        grid=(s // tb,),
        in_specs=[
            pl.BlockSpec((tb, LANE), lambda t: (t, 0)),
            pl.BlockSpec((1, LANE), lambda t: (0, 0)),
            pl.BlockSpec((1, LANE), lambda t: (0, 0)),
        ],
        out_specs=[
            pl.BlockSpec((tb, LANE), lambda t: (t, 0)),
            pl.BlockSpec((tb // CHUNK, LANE, CHUNK), lambda t: (t, 0, 0)),
        ],
        out_shape=[
            jax.ShapeDtypeStruct((s, LANE), F32),
            jax.ShapeDtypeStruct((s // CHUNK, LANE, CHUNK), F32),
        ],
        compiler_params=_params("parallel"),
        name="gdn_gates",
    )(extra, alog_row, dtb_row)


def _gdn_kernel(q_ref, k_ref, v_ref, z_ref, cols_ref, rows_ref, cwq_ref, cwk_ref, cwv_ref,
                on_ref, o_ref, qh_ref, kh_ref, vh_ref, qt_ref, kt_ref, vt_ref, st_ref, *, tb):
    hg = pl.program_id(0)
    t = pl.program_id(1)
    heads = range(GDN_GROUP)

    hists = ((qh_ref, qt_ref, q_ref), (kh_ref, kt_ref, k_ref), (vh_ref, vt_ref, v_ref))

    @pl.when(t == 0)
    def _():
        for _, tail_ref, _ in hists:
            tail_ref[...] = jnp.zeros_like(tail_ref)
        st_ref[...] = jnp.zeros_like(st_ref)

    for hist_ref, tail_ref, x_ref in hists:
        hist_ref[0:CONV_PAD_BF16, :] = tail_ref[...]
        for i in range(x_ref.shape[0]):
            hist_ref[CONV_PAD_BF16:CONV_PAD_BF16 + tb, i * LANE:(i + 1) * LANE] = x_ref[i]
    cwq, cwk, cwv = cwq_ref[...], cwk_ref[...], cwv_ref[...]

    ext = CONV_PAD_BF16 + CHUNK
    tap_row = _iota2((CONV_WIDTH * CHUNK, ext), 0)
    tap_col = _iota2((CONV_WIDTH * CHUNK, ext), 1)
    shift = (tap_col == tap_row % CHUNK + tap_row // CHUNK
             + (CONV_PAD_BF16 - (CONV_WIDTH - 1))).astype(BF16)

    def conv_silu(hist_ref, w, c):
        taps = jnp.dot(shift, hist_ref[c * CHUNK:c * CHUNK + ext, :], preferred_element_type=F32)
        acc = w[0:1, :] * taps[0:CHUNK]
        for kk in range(1, CONV_WIDTH):
            acc = acc + w[kk:kk + 1, :] * taps[kk * CHUNK:(kk + 1) * CHUNK]
        return _silu(acc)

    def l2n(x):
        return x * lax.rsqrt(jnp.sum(x * x, axis=-1, keepdims=True) + L2_EPS)

    lane = _iota2((CHUNK, LANE), 1)
    sub = _iota2((CHUNK, LANE), 0)
    left = lane < CHUNK
    col = jnp.where(left, lane, lane - CHUNK)
    incl = sub >= col
    strict = sub > col
    eye_p = (sub == col).astype(F32)
    on = on_ref[...]

    def block_diag(m):
        return jnp.concatenate([jnp.where(left, m, 0.0), jnp.where(left, 0.0, m)], axis=0)

    chunks = range(tb // CHUNK)
    units = [(j, c) for c in chunks for j in heads]
    vheads = range(GDN_REP)
    rows = [slice(c * CHUNK, (c + 1) * CHUNK) for c in chunks]
    q, k, v, beta, gcum, glast, decay, kq, xpw, tinv, attn = ({} for _ in range(11))
    qc = {c: conv_silu(qh_ref, cwq, c) for c in chunks}
    kc = {c: conv_silu(kh_ref, cwk, c) for c in chunks}
    v = {c: conv_silu(vh_ref, cwv, c) for c in chunks}
    for u in units:
        j, c = u
        q[u] = l2n(qc[c][:, j * LANE:(j + 1) * LANE]) * (GDN_DK ** -0.5)
        k[u] = l2n(kc[c][:, j * LANE:(j + 1) * LANE])
    for u in units:
        j, c = u
        cols = cols_ref[rows[c], :]
        hv = [(hg * GDN_GROUP + j) * GDN_REP + e for e in vheads]
        beta[u] = [jnp.sum(jnp.where(lane == h, cols, 0.0), axis=1, keepdims=True) for h in hv]
        gcum[u] = [jnp.sum(jnp.where(lane == h + GDN_V_HEADS, cols, 0.0), axis=1, keepdims=True)
                   for h in hv]
        grow = [rows_ref[c, pl.ds(GDN_V_HEADS + h, 1), :] for h in hv]
        grow_p = jnp.concatenate(grow, axis=1)
        glast[u] = [r[:, CHUNK - 1:CHUNK] for r in grow]
        gcol_p = jnp.where(left, gcum[u][0], gcum[u][1])
        decay[u] = jnp.where(incl, jnp.exp(jnp.where(incl, gcol_p - grow_p, 0.0)), 0.0)
    for u in units:
        kq[u] = _dot_nt(jnp.concatenate([k[u], q[u]], axis=0), jnp.concatenate([k[u], k[u]], axis=0))
    for u in units:
        beta_p = jnp.where(left, beta[u][0], beta[u][1])
        lmat = jnp.where(strict, kq[u][:CHUNK] * beta_p * decay[u], 0.0)
        attn[u] = kq[u][CHUNK:] * decay[u]
        xpw[u] = -lmat
        tinv[u] = eye_p + xpw[u]
    for u in units:
        xpw[u] = _dot(xpw[u], block_diag(xpw[u]))
    for _ in range(4):
        for u in units:
            both = _dot(jnp.concatenate([tinv[u], xpw[u]], axis=0), block_diag(xpw[u]))
            tinv[u] = tinv[u] + both[:CHUNK]
            xpw[u] = both[CHUNK:]
    for u in units:
        tinv[u] = block_diag(tinv[u] + _dot(tinv[u], block_diag(xpw[u])))
    sol, att_sol = {}, {}
    for u in units:
        j, c = u
        rhs = jnp.concatenate(
            [jnp.concatenate([v[c][:, (j * GDN_REP + e) * LANE:(j * GDN_REP + e + 1) * LANE]
                              * beta[u][e],
                              k[u] * (beta[u][e] * jnp.exp(gcum[u][e]))], axis=1)
             for e in vheads], axis=0)
        sol[u] = _dot(tinv[u], rhs)
    for u in units:
        att_sol[u] = _dot(block_diag(attn[u]), sol[u])
    lhs, upd = {}, {}
    for u in units:
        for e in vheads:
            hrows = slice(e * CHUNK, (e + 1) * CHUNK)
            kd = k[u] * jnp.exp(glast[u][e] - gcum[u][e])
            kd_sol = _dot_tn(kd, sol[u][hrows, :])
            qd = q[u] * jnp.exp(gcum[u][e]) - att_sol[u][hrows, GDN_DV:]
            lhs[u, e] = jnp.concatenate([kd_sol[:, GDN_DV:], qd], axis=0)
            upd[u, e] = kd_sol[:, :GDN_DV]
    for u in units:
        j, c = u
        for e in vheads:
            hrows = slice(e * CHUNK, (e + 1) * CHUNK)
            hv = j * GDN_REP + e
            st = st_ref[hv]
            prod = _dot(lhs[u, e], st)
            st_ref[hv] = jnp.exp(glast[u][e]) * st + upd[u, e] - prod[:GDN_DK]
            out = prod[GDN_DK:] + att_sol[u][hrows, :GDN_DV]
            zg = z_ref[hv, rows[c], :].astype(F32)
            o_ref[rows[c], hv * LANE:(hv + 1) * LANE] = (_rms(out, on) * _silu(zg)).astype(o_ref.dtype)
    for _, tail_ref, x_ref in hists:
        for i in range(x_ref.shape[0]):
            tail_ref[:, i * LANE:(i + 1) * LANE] = x_ref[i, tb - CONV_PAD_BF16:tb, :]


def _gdn(slabs, cols, rows, conv_w, out_norm, tb=256):
    s = slabs.shape[1]
    g = GDN_GROUP
    gv = GDN_GROUP * GDN_REP
    nq = GDN_QK_HEADS // g
    return pl.pallas_call(
        functools.partial(_gdn_kernel, tb=tb),
        grid=(nq, s // tb),
        in_specs=[
            pl.BlockSpec((g, tb, LANE), lambda h, t: (h, t, 0)),
            pl.BlockSpec((g, tb, LANE), lambda h, t: (nq + h, t, 0)),
            pl.BlockSpec((gv, tb, LANE), lambda h, t: (nq + h, t, 0)),
            pl.BlockSpec((gv, tb, LANE), lambda h, t: (2 * nq + h, t, 0)),
            pl.BlockSpec((tb, LANE), lambda h, t: (t, 0)),
            pl.BlockSpec((tb // CHUNK, LANE, CHUNK), lambda h, t: (t, 0, 0)),
            pl.BlockSpec((CONV_WIDTH, g * LANE), lambda h, t: (0, h)),
            pl.BlockSpec((CONV_WIDTH, g * LANE), lambda h, t: (0, nq + h)),
            pl.BlockSpec((CONV_WIDTH, gv * LANE), lambda h, t: (0, nq + h)),
            pl.BlockSpec((1, GDN_DV), lambda h, t: (0, 0)),
        ],
        out_specs=pl.BlockSpec((tb, gv * LANE), lambda h, t: (t, h)),
        out_shape=jax.ShapeDtypeStruct((s, GDN_VAL), BF16),
        scratch_shapes=[
            pltpu.VMEM((tb + CONV_PAD_BF16, g * LANE), BF16),
            pltpu.VMEM((tb + CONV_PAD_BF16, g * LANE), BF16),
            pltpu.VMEM((tb + CONV_PAD_BF16, gv * LANE), BF16),
            pltpu.VMEM((CONV_PAD_BF16, g * LANE), BF16),
            pltpu.VMEM((CONV_PAD_BF16, g * LANE), BF16),
            pltpu.VMEM((CONV_PAD_BF16, gv * LANE), BF16),
            pltpu.VMEM((gv, GDN_DK, GDN_DV), F32),
        ],
        compiler_params=_params("parallel", "arbitrary"),
        name="gdn",
    )(slabs, slabs, slabs, slabs, cols, rows, conv_w, conv_w, conv_w, out_norm.reshape(1, -1))


def _pad_cols(w, n):
    return jnp.pad(w, ((0, 0), (0, n - w.shape[1])))


def _mixer_a(h, gain, w_in, gate_w2, gate_b, gla_norm, conv_w, conv_b, w_a, b_a, w_x, b_x, lam, w_out):
    c_lr = 2 * GLA_KEY + 2 * GLA_VAL
    c_x = c_lr + GLA_GATE_RANK
    w_main = jnp.concatenate([w_in[:, :c_lr], w_in[:, c_x:]], axis=1)
    w_extra = _pad_cols(w_in[:, c_lr:c_x], LANE).astype(BF16)
    slabs, extra = _norm_proj(h, gain, w_main, w_extra)
    w2 = jnp.pad(gate_w2, ((0, LANE - GLA_GATE_RANK), (0, 0)))
    o_gla = _gla(slabs, extra, w2, gate_b.reshape(1, -1), gla_norm.reshape(1, -1))
    x_slab0 = c_lr // LANE
    o_lru = _lru(slabs, x_slab0, x_slab0 + LRU_WIDTH // LANE, conv_w, conv_b, w_a, b_a, w_x, b_x, lam)
    w_out = w_out.astype(BF16)
    return _proj_res(h, [o_gla, o_lru], [w_out[:GLA_VAL], w_out[GLA_VAL:]])


def _mixer_c(h, gain, w_in, conv_w, a_log, dt_bias, out_norm, w_out):
    c_main = 2 * GDN_KEY + 2 * GDN_VAL
    w_extra = _pad_cols(w_in[:, c_main:], LANE).astype(BF16)
    slabs, extra = _norm_proj(h, gain, w_in, w_extra, n=c_main)
    cols, rows = _gdn_gates(extra, a_log, dt_bias)
    o = _gdn(slabs, cols, rows, conv_w, out_norm)
    return _proj_res(h, [o], [w_out.astype(BF16)])


def kernel(x, norms, ffn_w_gate, ffn_w_up, ffn_w_down, a_w_in, a_gla_gate_w2, a_gla_gate_b, a_gla_norm, a_rg_conv_w, a_rg_conv_b, a_rg_w_a, a_rg_b_a, a_rg_w_x, a_rg_b_x, a_rg_lambda, a_w_out, c_w_in, c_conv_w, c_a_log, c_dt_bias, c_out_norm, c_w_out, final_norm):
    b, s, d = x.shape
    depth = norms.shape[0]
    wg, wu, wd = ffn_w_gate, ffn_w_up, ffn_w_down
    outs = []
    for bi in range(b):
        h = x.reshape(s, d) if b == 1 else x[bi]
        for layer in range(depth):
            j = layer // 2
            h = _ffn(h, norms[layer, 0], wg, wu, wd, layer, 0)
            if layer % 2 == 0:
                h = _mixer_a(h, norms[layer, 1], a_w_in[j], a_gla_gate_w2[j], a_gla_gate_b[j],
                             a_gla_norm[j], a_rg_conv_w[j], a_rg_conv_b[j], a_rg_w_a[j], a_rg_b_a[j],
                             a_rg_w_x[j], a_rg_b_x[j], a_rg_lambda[j], a_w_out[j])
            else:
                h = _mixer_c(h, norms[layer, 1], c_w_in[j], c_conv_w[j], c_a_log[j], c_dt_bias[j],
                             c_out_norm[j], c_w_out[j])
            last = layer == depth - 1
            h = _ffn(h, norms[layer, 2], wg, wu, wd, layer, 1,
                     final_gain=final_norm if last else None)
        outs.append(h)
    return outs[0].reshape(b, s, d) if b == 1 else jnp.stack(outs, axis=0)
```

```python
import functools

import jax
import jax.numpy as jnp
from jax import lax
from jax.experimental import pallas as pl
from jax.experimental.pallas import tpu as pltpu

F32 = jnp.float32
BF16 = jnp.bfloat16
HIGHEST = lax.Precision.HIGHEST

LANE = 128
D_MODEL = 2048
D_FF = 5632
RMS_EPS = 1e-6
L2_EPS = 1e-6
CONV_WIDTH = 4
CONV_PAD = 8
CONV_PAD_BF16 = 16

GLA_HEADS = 4
GLA_DK = 128
GLA_DV = 256
GLA_KEY = GLA_HEADS * GLA_DK
GLA_VAL = GLA_HEADS * GLA_DV
GLA_GATE_RANK = 16
GLA_GATE_NORMALIZER = 16.0
CHUNK = 64

LRU_WIDTH = 1024
LRU_BLOCKS = 8
LRU_C = 8.0

GDN_QK_HEADS = 16
GDN_V_HEADS = 32
GDN_REP = GDN_V_HEADS // GDN_QK_HEADS
GDN_GROUP = 4
GDN_DK = 128
GDN_DV = 128
GDN_KEY = GDN_QK_HEADS * GDN_DK
GDN_VAL = GDN_V_HEADS * GDN_DV

VMEM_LIMIT = 56 * 1024 * 1024


def _params(*sem):
    return pltpu.CompilerParams(dimension_semantics=sem, vmem_limit_bytes=VMEM_LIMIT)


def _rms(x, gain):
    return x * lax.rsqrt(jnp.mean(x * x, axis=-1, keepdims=True) + RMS_EPS) * gain


def _softplus(z):
    return jnp.maximum(z, 0.0) + jnp.log1p(jnp.exp(-jnp.abs(z)))


def _silu(z):
    return z * jax.nn.sigmoid(z)


def _dot(a, b):
    return jnp.dot(a.astype(BF16), b.astype(BF16), preferred_element_type=F32)


def _dot_nt(a, b):
    return lax.dot_general(a.astype(BF16), b.astype(BF16), (((1,), (1,)), ((), ())),
                           preferred_element_type=F32)


def _dot_tn(a, b):
    return lax.dot_general(a.astype(BF16), b.astype(BF16), (((0,), (0,)), ((), ())),
                           preferred_element_type=F32)


def _dot_f32(a, b):
    return jnp.dot(a, b, precision=HIGHEST, preferred_element_type=F32)


def _iota2(shape, dim):
    return lax.broadcasted_iota(jnp.int32, shape, dim)


def _ffn_kernel(x_ref, g_ref, wg_ref, wu_ref, wd_ref, fg_ref, o_ref, n_ref, *, final_norm):
    j = pl.program_id(1)

    @pl.when(j == 0)
    def _():
        x = x_ref[...]
        n_ref[...] = _rms(x, g_ref[...]).astype(BF16)
        o_ref[...] = x

    n = n_ref[...]
    gate = jnp.dot(n, wg_ref[...].astype(BF16), preferred_element_type=F32)
    up = jnp.dot(n, wu_ref[...].astype(BF16), preferred_element_type=F32)
    act = (0.5 * _silu(gate) * up).astype(BF16)
    o_ref[...] += jnp.dot(act, wd_ref[...].astype(BF16), preferred_element_type=F32)

    if final_norm:
        @pl.when(j == pl.num_programs(1) - 1)
        def _():
            o_ref[...] = _rms(o_ref[...], fg_ref[...])


def _ffn(h, gain, wg, wu, wd, layer, idx, final_gain=None, tm=1024, tf=256):
    s, d = h.shape
    dff = wg.shape[-1]
    fg = jnp.ones((1, d), F32) if final_gain is None else final_gain.reshape(1, d)
    return pl.pallas_call(
        functools.partial(_ffn_kernel, final_norm=final_gain is not None),
        grid=(s // tm, dff // tf),
        in_specs=[
            pl.BlockSpec((tm, d), lambda i, j: (i, 0)),
            pl.BlockSpec((1, d), lambda i, j: (0, 0)),
            pl.BlockSpec((None, None, d, tf), lambda i, j: (layer, idx, 0, j)),
            pl.BlockSpec((None, None, d, tf), lambda i, j: (layer, idx, 0, j)),
            pl.BlockSpec((None, None, tf, d), lambda i, j: (layer, idx, j, 0)),
            pl.BlockSpec((1, d), lambda i, j: (0, 0)),
        ],
        out_specs=pl.BlockSpec((tm, d), lambda i, j: (i, 0)),
        out_shape=jax.ShapeDtypeStruct((s, d), F32),
        scratch_shapes=[pltpu.VMEM((tm, d), BF16)],
        compiler_params=_params("parallel", "arbitrary"),
        name="ffn",
    )(h, gain.reshape(1, d), wg, wu, wd, fg)


def _norm_proj_kernel(x_ref, g_ref, w_ref, we_ref, o_ref, e_ref, n_ref):
    j = pl.program_id(1)

    @pl.when(j == 0)
    def _():
        nb = _rms(x_ref[...], g_ref[...]).astype(BF16)
        n_ref[...] = nb
        e_ref[...] = jnp.dot(nb, we_ref[...], preferred_element_type=F32)

    acc = jnp.dot(n_ref[...], w_ref[...], preferred_element_type=F32)
    for c in range(o_ref.shape[0]):
        o_ref[c] = acc[:, c * LANE:(c + 1) * LANE].astype(o_ref.dtype)


def _norm_proj(h, gain, w_main, w_extra, n=None, tm=1024, tn=1024):
    s, d = h.shape
    n = w_main.shape[1] if n is None else n
    return pl.pallas_call(
        _norm_proj_kernel,
        grid=(s // tm, n // tn),
        in_specs=[
            pl.BlockSpec((tm, d), lambda i, j: (i, 0)),
            pl.BlockSpec((1, d), lambda i, j: (0, 0)),
            pl.BlockSpec((d, tn), lambda i, j: (0, j)),
            pl.BlockSpec((d, LANE), lambda i, j: (0, 0)),
        ],
        out_specs=[
            pl.BlockSpec((tn // LANE, tm, LANE), lambda i, j: (j, i, 0)),
            pl.BlockSpec((tm, LANE), lambda i, j: (i, 0)),
        ],
        out_shape=[
            jax.ShapeDtypeStruct((n // LANE, s, LANE), BF16),
            jax.ShapeDtypeStruct((s, LANE), F32),
        ],
        scratch_shapes=[pltpu.VMEM((tm, d), BF16)],
        compiler_params=_params("parallel", "arbitrary"),
        name="norm_proj",
    )(h, gain.reshape(1, d), w_main, w_extra)


def _proj_res_kernel(*refs, n_in):
    h_ref = refs[0]
    x_refs = refs[1:1 + n_in]
    w_refs = refs[1 + n_in:1 + 2 * n_in]
    o_ref = refs[1 + 2 * n_in]
    acc = h_ref[...]
    for x_ref, w_ref in zip(x_refs, w_refs):
        acc = acc + jnp.dot(x_ref[...], w_ref[...], preferred_element_type=F32)
    o_ref[...] = acc


def _proj_res(h, xs, ws, tm=1024, tn=1024):
    s, d = h.shape
    n_in = len(xs)
    in_specs = [pl.BlockSpec((tm, tn), lambda i, j: (i, j))]
    in_specs += [pl.BlockSpec((tm, x.shape[1]), lambda i, j: (i, 0)) for x in xs]
    in_specs += [pl.BlockSpec((w.shape[0], tn), lambda i, j: (0, j)) for w in ws]
    return pl.pallas_call(
        functools.partial(_proj_res_kernel, n_in=n_in),
        grid=(s // tm, d // tn),
        in_specs=in_specs,
        out_specs=pl.BlockSpec((tm, tn), lambda i, j: (i, j)),
        out_shape=jax.ShapeDtypeStruct((s, d), F32),
        compiler_params=_params("parallel", "arbitrary"),
        name="proj_res",
    )(h, *xs, *ws)


def _gla_kernel(q_ref, k_ref, v_ref, og_ref, lr_ref, w2_ref, b_ref, gn_ref, o_ref, st_ref, *, tb):
    t = pl.program_id(0)

    @pl.when(t == 0)
    def _():
        st_ref[...] = jnp.zeros_like(st_ref)

    tril = (_iota2((CHUNK, CHUNK), 0) >= _iota2((CHUNK, CHUNK), 1))
    tril_f = tril.astype(F32)
    gn = gn_ref[...]
    vs = GLA_DV // LANE
    z = _dot(lr_ref[...], w2_ref[...]) + b_ref[...]
    log_a = -_softplus(-z) / GLA_GATE_NORMALIZER
    chunks = range(tb // CHUNK)
    rows = [slice(c * CHUNK, (c + 1) * CHUNK) for c in chunks]
    units = [(h, c) for c in chunks for h in range(GLA_HEADS)]
    cum_all = [_dot_f32(tril_f, log_a[rows[c], :]) for c in chunks]
    intra, qe, upd, dec = {}, {}, {}, {}
    for u in units:
        h, c = u
        cum = cum_all[c][:, h * GLA_DK:(h + 1) * GLA_DK]
        mid = cum[CHUNK // 2 - 1:CHUNK // 2, :]
        last = cum[CHUNK - 1:CHUNK, :]
        q = q_ref[h, rows[c], :].astype(F32) * (GLA_DK ** -0.5)
        k = k_ref[h, rows[c], :].astype(F32)
        v = jnp.concatenate([v_ref[vs * h + i, rows[c], :] for i in range(vs)], axis=1)
        scores = _dot_nt(q * jnp.exp(cum - mid), k * jnp.exp(mid - cum))
        intra[u] = _dot(jnp.where(tril, scores, 0.0), v)
        qe[u] = q * jnp.exp(cum)
        upd[u] = _dot_tn(v, k * jnp.exp(last - cum))
        dec[u] = jnp.exp(last)
    for u in units:
        h, c = u
        st = st_ref[h]
        out = intra[u] + _dot_nt(qe[u], st)
        st_ref[h] = dec[u] * st + upd[u]
        og = jnp.concatenate([og_ref[vs * h + i, rows[c], :] for i in range(vs)], axis=1).astype(F32)
        o_ref[rows[c], h * GLA_DV:(h + 1) * GLA_DV] = (_rms(out, gn) * _silu(og)).astype(o_ref.dtype)


def _gla(slabs, extra, w2, bias, gnorm, tb=256):
    s = slabs.shape[1]
    nk = GLA_KEY // LANE
    return pl.pallas_call(
        functools.partial(_gla_kernel, tb=tb),
        grid=(s // tb,),
        in_specs=[
            pl.BlockSpec((nk, tb, LANE), lambda t: (0, t, 0)),
            pl.BlockSpec((nk, tb, LANE), lambda t: (1, t, 0)),
            pl.BlockSpec((2 * nk, tb, LANE), lambda t: (1, t, 0)),
            pl.BlockSpec((2 * nk, tb, LANE), lambda t: (2, t, 0)),
            pl.BlockSpec((tb, LANE), lambda t: (t, 0)),
            pl.BlockSpec((LANE, GLA_KEY), lambda t: (0, 0)),
            pl.BlockSpec((1, GLA_KEY), lambda t: (0, 0)),
            pl.BlockSpec((1, GLA_DV), lambda t: (0, 0)),
        ],
        out_specs=pl.BlockSpec((tb, GLA_VAL), lambda t: (t, 0)),
        out_shape=jax.ShapeDtypeStruct((s, GLA_VAL), BF16),
        scratch_shapes=[pltpu.VMEM((GLA_HEADS, GLA_DV, GLA_DK), F32)],
        compiler_params=_params("arbitrary"),
        name="gla",
    )(slabs, slabs, slabs, slabs, extra, w2, bias, gnorm)


def _conv_rows(hist_ref, w, r0, n):
    acc = None
    for kk in range(CONV_WIDTH):
        off = CONV_PAD + r0 - (CONV_WIDTH - 1) + kk
        term = w[kk:kk + 1, :] * hist_ref[off:off + n, :]
        acc = term if acc is None else acc + term
    return acc


def _conv_carry(hist_ref, tb):
    hist_ref[0:CONV_PAD, :] = hist_ref[tb:tb + CONV_PAD, :]


def _causal_conv(hist_ref, x, w, tb):
    hist_ref[CONV_PAD:CONV_PAD + tb, :] = x
    acc = _conv_rows(hist_ref, w, 0, tb)
    _conv_carry(hist_ref, tb)
    return acc


def _lru_kernel(x_ref, y_ref, cw_ref, cb_ref, wa_ref, ba_ref, wx_ref, bx_ref, lam_ref,
                o_ref, hist_ref, h_ref, *, tb):
    t = pl.program_id(1)

    @pl.when(t == 0)
    def _():
        hist_ref[0:CONV_PAD, :] = jnp.zeros((CONV_PAD, LANE), F32)
        h_ref[...] = jnp.zeros_like(h_ref)

    xc = _causal_conv(hist_ref, x_ref[0].astype(F32), cw_ref[...], tb) + cb_ref[...]
    r = jax.nn.sigmoid(_dot(xc, wa_ref[0]) + ba_ref[...])
    i = jax.nn.sigmoid(_dot(xc, wx_ref[0]) + bx_ref[...])
    log_a = -LRU_C * r * _softplus(-lam_ref[...])
    a = jnp.exp(log_a)
    u = jnp.sqrt(-jnp.tanh(log_a) * (a * a + 1.0)) * (i * xc)

    row = _iota2((tb, LANE), 0)
    sh = 1
    while sh < tb:
        keep = row >= sh
        a_prev = jnp.where(keep, pltpu.roll(a, sh, 0), 1.0)
        u_prev = jnp.where(keep, pltpu.roll(u, sh, 0), 0.0)
        u = u + a * u_prev
        a = a * a_prev
        sh *= 2
    h = u + a * h_ref[0:1, :]
    h_ref[...] = jnp.broadcast_to(h[tb - 1:tb, :], h_ref.shape)
    o_ref[...] = (h * jax.nn.gelu(y_ref[0].astype(F32))).astype(o_ref.dtype)


def _lru(slabs, x_slab0, y_slab0, conv_w, conv_b, w_a, b_a, w_x, b_x, lam, tb=512):
    s = slabs.shape[1]
    row = lambda n, t: (0, n)
    return pl.pallas_call(
        functools.partial(_lru_kernel, tb=tb),
        grid=(LRU_BLOCKS, s // tb),
        in_specs=[
            pl.BlockSpec((1, tb, LANE), lambda n, t: (x_slab0 + n, t, 0)),
            pl.BlockSpec((1, tb, LANE), lambda n, t: (y_slab0 + n, t, 0)),
            pl.BlockSpec((CONV_WIDTH, LANE), row),
            pl.BlockSpec((1, LANE), row),
            pl.BlockSpec((1, LANE, LANE), lambda n, t: (n, 0, 0)),
            pl.BlockSpec((1, LANE), row),
            pl.BlockSpec((1, LANE, LANE), lambda n, t: (n, 0, 0)),
            pl.BlockSpec((1, LANE), row),
            pl.BlockSpec((1, LANE), row),
        ],
        out_specs=pl.BlockSpec((tb, LANE), lambda n, t: (t, n)),
        out_shape=jax.ShapeDtypeStruct((s, LRU_WIDTH), BF16),
        scratch_shapes=[pltpu.VMEM((tb + CONV_PAD, LANE), F32), pltpu.VMEM((8, LANE), F32)],
        compiler_params=_params("parallel", "arbitrary"),
        name="rg_lru",
    )(slabs, slabs, conv_w, conv_b.reshape(1, -1), w_a, b_a.reshape(1, -1), w_x,
      b_x.reshape(1, -1), lam.reshape(1, -1))


def _gdn_gates_kernel(e_ref, alog_ref, dtb_ref, cols_ref, rows_ref, *, tb):
    ex = e_ref[...]
    is_beta = _iota2((CHUNK, LANE), 1) < GDN_V_HEADS
    beta = jax.nn.sigmoid(ex)
    g = -jnp.exp(alog_ref[...]) * _softplus(ex + dtb_ref[...])
    tril_f = (_iota2((CHUNK, CHUNK), 0) >= _iota2((CHUNK, CHUNK), 1)).astype(F32)
    for c in range(tb // CHUNK):
        rows = slice(c * CHUNK, (c + 1) * CHUNK)
        gcum = _dot_f32(tril_f, g[rows, :])
        cols_ref[rows, :] = jnp.where(is_beta, beta[rows, :], gcum)
        rows_ref[c] = gcum.T


def _gdn_gates(extra, a_log, dt_bias, tb=512):
    s = extra.shape[0]
    alog_row = jnp.zeros((1, LANE), F32).at[0, GDN_V_HEADS:2 * GDN_V_HEADS].set(a_log)
    dtb_row = jnp.zeros((1, LANE), F32).at[0, GDN_V_HEADS:2 * GDN_V_HEADS].set(dt_bias)
    return pl.pallas_call(
        functools.partial(_gdn_gates_kernel, tb=tb),
        grid=(s // tb,),
        in_specs=[
            pl.BlockSpec((tb, LANE), lambda t: (t, 0)),
            pl.BlockSpec((1, LANE), lambda t: (0, 0)),
            pl.BlockSpec((1, LANE), lambda t: (0, 0)),
        ],
        out_specs=[
            pl.BlockSpec((tb, LANE), lambda t: (t, 0)),
            pl.BlockSpec((tb // CHUNK, LANE, CHUNK), lambda t: (t, 0, 0)),
        ],
        out_shape=[
            jax.ShapeDtypeStruct((s, LANE), F32),
            jax.ShapeDtypeStruct((s // CHUNK, LANE, CHUNK), F32),
        ],
        compiler_params=_params("parallel"),
        name="gdn_gates",
    )(extra, alog_row, dtb_row)


def _gdn_kernel(q_ref, k_ref, v_ref, z_ref, cols_ref, rows_ref, cwq_ref, cwk_ref, cwv_ref,
                on_ref, o_ref, qh_ref, kh_ref, vh_ref, qt_ref, kt_ref, vt_ref, st_ref, *, tb):
    hg = pl.program_id(0)
    t = pl.program_id(1)
    heads = range(GDN_GROUP)

    hists = ((qh_ref, qt_ref, q_ref), (kh_ref, kt_ref, k_ref), (vh_ref, vt_ref, v_ref))

    @pl.when(t == 0)
    def _():
        for _, tail_ref, _ in hists:
            tail_ref[...] = jnp.zeros_like(tail_ref)
        st_ref[...] = jnp.zeros_like(st_ref)

    for hist_ref, tail_ref, x_ref in hists:
        hist_ref[0:CONV_PAD_BF16, :] = tail_ref[...]
        for i in range(x_ref.shape[0]):
            hist_ref[CONV_PAD_BF16:CONV_PAD_BF16 + tb, i * LANE:(i + 1) * LANE] = x_ref[i]
    cwq, cwk, cwv = cwq_ref[...], cwk_ref[...], cwv_ref[...]

    ext = CONV_PAD_BF16 + CHUNK
    tap_row = _iota2((CONV_WIDTH * CHUNK, ext), 0)
    tap_col = _iota2((CONV_WIDTH * CHUNK, ext), 1)
    shift = (tap_col == tap_row % CHUNK + tap_row // CHUNK
             + (CONV_PAD_BF16 - (CONV_WIDTH - 1))).astype(BF16)

    def conv_silu(hist_ref, w, c):
        taps = jnp.dot(shift, hist_ref[c * CHUNK:c * CHUNK + ext, :], preferred_element_type=F32)
        acc = w[0:1, :] * taps[0:CHUNK]
        for kk in range(1, CONV_WIDTH):
            acc = acc + w[kk:kk + 1, :] * taps[kk * CHUNK:(kk + 1) * CHUNK]
        return _silu(acc)

    def l2n(x):
        return x * lax.rsqrt(jnp.sum(x * x, axis=-1, keepdims=True) + L2_EPS)

    lane = _iota2((CHUNK, LANE), 1)
    sub = _iota2((CHUNK, LANE), 0)
    left = lane < CHUNK
    col = jnp.where(left, lane, lane - CHUNK)
    incl = sub >= col
    strict = sub > col
    eye_p = (sub == col).astype(F32)
    on = on_ref[...]

    def block_diag(m):
        return jnp.concatenate([jnp.where(left, m, 0.0), jnp.where(left, 0.0, m)], axis=0)

    chunks = range(tb // CHUNK)
    units = [(j, c) for c in chunks for j in heads]
    vheads = range(GDN_REP)
    rows = [slice(c * CHUNK, (c + 1) * CHUNK) for c in chunks]
    q, k, v, beta, gcum, glast, decay, kq, xpw, tinv, attn = ({} for _ in range(11))
    qc = {c: conv_silu(qh_ref, cwq, c) for c in chunks}
    kc = {c: conv_silu(kh_ref, cwk, c) for c in chunks}
    v = {c: conv_silu(vh_ref, cwv, c) for c in chunks}
    for u in units:
        j, c = u
        q[u] = l2n(qc[c][:, j * LANE:(j + 1) * LANE]) * (GDN_DK ** -0.5)
        k[u] = l2n(kc[c][:, j * LANE:(j + 1) * LANE])
    for u in units:
        j, c = u
        cols = cols_ref[rows[c], :]
        hv = [(hg * GDN_GROUP + j) * GDN_REP + e for e in vheads]
        beta[u] = [jnp.sum(jnp.where(lane == h, cols, 0.0), axis=1, keepdims=True) for h in hv]
        gcum[u] = [jnp.sum(jnp.where(lane == h + GDN_V_HEADS, cols, 0.0), axis=1, keepdims=True)
                   for h in hv]
        grow = [rows_ref[c, pl.ds(GDN_V_HEADS + h, 1), :] for h in hv]
        grow_p = jnp.concatenate(grow, axis=1)
        glast[u] = [r[:, CHUNK - 1:CHUNK] for r in grow]
        gcol_p = jnp.where(left, gcum[u][0], gcum[u][1])
        decay[u] = jnp.where(incl, jnp.exp(jnp.where(incl, gcol_p - grow_p, 0.0)), 0.0)
    for u in units:
        kq[u] = _dot_nt(jnp.concatenate([k[u], q[u]], axis=0), jnp.concatenate([k[u], k[u]], axis=0))
    for u in units:
        beta_p = jnp.where(left, beta[u][0], beta[u][1])
        lmat = jnp.where(strict, kq[u][:CHUNK] * beta_p * decay[u], 0.0)
        attn[u] = kq[u][CHUNK:] * decay[u]
        xpw[u] = -lmat
        tinv[u] = eye_p + xpw[u]
    for u in units:
        xpw[u] = _dot(xpw[u], block_diag(xpw[u]))
    for _ in range(4):
        for u in units:
            both = _dot(jnp.concatenate([tinv[u], xpw[u]], axis=0), block_diag(xpw[u]))
            tinv[u] = tinv[u] + both[:CHUNK]
            xpw[u] = both[CHUNK:]
    for u in units:
        tinv[u] = block_diag(tinv[u] + _dot(tinv[u], block_diag(xpw[u])))
    sol, att_sol = {}, {}
    for u in units:
        j, c = u
        rhs = jnp.concatenate(
            [jnp.concatenate([v[c][:, (j * GDN_REP + e) * LANE:(j * GDN_REP + e + 1) * LANE]
                              * beta[u][e],
                              k[u] * (beta[u][e] * jnp.exp(gcum[u][e]))], axis=1)
             for e in vheads], axis=0)
        sol[u] = _dot(tinv[u], rhs)
    for u in units:
        att_sol[u] = _dot(block_diag(attn[u]), sol[u])
    lhs, upd = {}, {}
    for u in units:
        for e in vheads:
            hrows = slice(e * CHUNK, (e + 1) * CHUNK)
            kd = k[u] * jnp.exp(glast[u][e] - gcum[u][e])
            kd_sol = _dot_tn(kd, sol[u][hrows, :])
            qd = q[u] * jnp.exp(gcum[u][e]) - att_sol[u][hrows, GDN_DV:]
            lhs[u, e] = jnp.concatenate([kd_sol[:, GDN_DV:], qd], axis=0)
            upd[u, e] = kd_sol[:, :GDN_DV]
    for u in units:
        j, c = u
        for e in vheads:
            hrows = slice(e * CHUNK, (e + 1) * CHUNK)
            hv = j * GDN_REP + e
            st = st_ref[hv]
            prod = _dot(lhs[u, e], st)
            st_ref[hv] = jnp.exp(glast[u][e]) * st + upd[u, e] - prod[:GDN_DK]
            out = prod[GDN_DK:] + att_sol[u][hrows, :GDN_DV]
            zg = z_ref[hv, rows[c], :].astype(F32)
            o_ref[rows[c], hv * LANE:(hv + 1) * LANE] = (_rms(out, on) * _silu(zg)).astype(o_ref.dtype)
    for _, tail_ref, x_ref in hists:
        for i in range(x_ref.shape[0]):
            tail_ref[:, i * LANE:(i + 1) * LANE] = x_ref[i, tb - CONV_PAD_BF16:tb, :]


def _gdn(slabs, cols, rows, conv_w, out_norm, tb=512):
    s = slabs.shape[1]
    g = GDN_GROUP
    gv = GDN_GROUP * GDN_REP
    nq = GDN_QK_HEADS // g
    return pl.pallas_call(
        functools.partial(_gdn_kernel, tb=tb),
        grid=(nq, s // tb),
        in_specs=[
            pl.BlockSpec((g, tb, LANE), lambda h, t: (h, t, 0)),
            pl.BlockSpec((g, tb, LANE), lambda h, t: (nq + h, t, 0)),
            pl.BlockSpec((gv, tb, LANE), lambda h, t: (nq + h, t, 0)),
            pl.BlockSpec((gv, tb, LANE), lambda h, t: (2 * nq + h, t, 0)),
            pl.BlockSpec((tb, LANE), lambda h, t: (t, 0)),
            pl.BlockSpec((tb // CHUNK, LANE, CHUNK), lambda h, t: (t, 0, 0)),
            pl.BlockSpec((CONV_WIDTH, g * LANE), lambda h, t: (0, h)),
            pl.BlockSpec((CONV_WIDTH, g * LANE), lambda h, t: (0, nq + h)),
            pl.BlockSpec((CONV_WIDTH, gv * LANE), lambda h, t: (0, nq + h)),
            pl.BlockSpec((1, GDN_DV), lambda h, t: (0, 0)),
        ],
        out_specs=pl.BlockSpec((tb, gv * LANE), lambda h, t: (t, h)),
        out_shape=jax.ShapeDtypeStruct((s, GDN_VAL), BF16),
        scratch_shapes=[
            pltpu.VMEM((tb + CONV_PAD_BF16, g * LANE), BF16),
            pltpu.VMEM((tb + CONV_PAD_BF16, g * LANE), BF16),
            pltpu.VMEM((tb + CONV_PAD_BF16, gv * LANE), BF16),
            pltpu.VMEM((CONV_PAD_BF16, g * LANE), BF16),
            pltpu.VMEM((CONV_PAD_BF16, g * LANE), BF16),
            pltpu.VMEM((CONV_PAD_BF16, gv * LANE), BF16),
            pltpu.VMEM((gv, GDN_DK, GDN_DV), F32),
        ],
        compiler_params=_params("parallel", "arbitrary"),
        name="gdn",
    )(slabs, slabs, slabs, slabs, cols, rows, conv_w, conv_w, conv_w, out_norm.reshape(1, -1))


def _pad_cols(w, n):
    return jnp.pad(w, ((0, 0), (0, n - w.shape[1])))


def _mixer_a(h, gain, w_in, gate_w2, gate_b, gla_norm, conv_w, conv_b, w_a, b_a, w_x, b_x, lam, w_out):
    c_lr = 2 * GLA_KEY + 2 * GLA_VAL
    c_x = c_lr + GLA_GATE_RANK
    w_main = jnp.concatenate([w_in[:, :c_lr], w_in[:, c_x:]], axis=1).astype(BF16)
    w_extra = _pad_cols(w_in[:, c_lr:c_x], LANE).astype(BF16)
    slabs, extra = _norm_proj(h, gain, w_main, w_extra)
    w2 = jnp.pad(gate_w2, ((0, LANE - GLA_GATE_RANK), (0, 0)))
    o_gla = _gla(slabs, extra, w2, gate_b.reshape(1, -1), gla_norm.reshape(1, -1))
    x_slab0 = c_lr // LANE
    o_lru = _lru(slabs, x_slab0, x_slab0 + LRU_WIDTH // LANE, conv_w, conv_b, w_a, b_a, w_x, b_x, lam)
    w_out = w_out.astype(BF16)
    return _proj_res(h, [o_gla, o_lru], [w_out[:GLA_VAL], w_out[GLA_VAL:]])


def _mixer_c(h, gain, w_in, conv_w, a_log, dt_bias, out_norm, w_out):
    c_main = 2 * GDN_KEY + 2 * GDN_VAL
    w_extra = _pad_cols(w_in[:, c_main:], LANE).astype(BF16)
    slabs, extra = _norm_proj(h, gain, w_in.astype(BF16), w_extra, n=c_main)
    cols, rows = _gdn_gates(extra, a_log, dt_bias)
    o = _gdn(slabs, cols, rows, conv_w, out_norm)
    return _proj_res(h, [o], [w_out.astype(BF16)])


def kernel(x, norms, ffn_w_gate, ffn_w_up, ffn_w_down, a_w_in, a_gla_gate_w2, a_gla_gate_b, a_gla_norm, a_rg_conv_w, a_rg_conv_b, a_rg_w_a, a_rg_b_a, a_rg_w_x, a_rg_b_x, a_rg_lambda, a_w_out, c_w_in, c_conv_w, c_a_log, c_dt_bias, c_out_norm, c_w_out, final_norm):
    b, s, d = x.shape
    depth = norms.shape[0]
    wg, wu, wd = ffn_w_gate, ffn_w_up, ffn_w_down
    outs = []
    for bi in range(b):
        h = x.reshape(s, d) if b == 1 else x[bi]
        for layer in range(depth):
            j = layer // 2
            h = _ffn(h, norms[layer, 0], wg, wu, wd, layer, 0)
            if layer % 2 == 0:
                h = _mixer_a(h, norms[layer, 1], a_w_in[j], a_gla_gate_w2[j], a_gla_gate_b[j],
                             a_gla_norm[j], a_rg_conv_w[j], a_rg_conv_b[j], a_rg_w_a[j], a_rg_b_a[j],
                             a_rg_w_x[j], a_rg_b_x[j], a_rg_lambda[j], a_w_out[j])
            else:
                h = _mixer_c(h, norms[layer, 1], c_w_in[j], c_conv_w[j], c_a_log[j], c_dt_bias[j],
                             c_out_norm[j], c_w_out[j])
            last = layer == depth - 1
            h = _ffn(h, norms[layer, 2], wg, wu, wd, layer, 1,
                     final_gain=final_norm if last else None)
        outs.append(h)
    return outs[0].reshape(b, s, d) if b == 1 else jnp.stack(outs, axis=0)
```

```python
import functools

import jax
import jax.numpy as jnp
from jax import lax
from jax.experimental import pallas as pl
from jax.experimental.pallas import tpu as pltpu

F32 = jnp.float32
BF16 = jnp.bfloat16
HIGHEST = lax.Precision.HIGHEST

LANE = 128
D_MODEL = 2048
D_FF = 5632
RMS_EPS = 1e-6
L2_EPS = 1e-6
CONV_WIDTH = 4
SUBLANE = 8
CONV_PAD = SUBLANE
CONV_PAD_BF16 = 16

GLA_HEADS = 4
GLA_DK = 128
GLA_DV = 256
GLA_KEY = GLA_HEADS * GLA_DK
GLA_VAL = GLA_HEADS * GLA_DV
GLA_GATE_RANK = 16
GLA_GATE_NORMALIZER = 16.0
CHUNK = 64

LRU_WIDTH = 1024
LRU_BLOCKS = 8
LRU_C = 8.0

GDN_QK_HEADS = 16
GDN_V_HEADS = 32
GDN_REP = GDN_V_HEADS // GDN_QK_HEADS
GDN_GROUP = 4
GDN_DK = 128
GDN_DV = 128
GDN_KEY = GDN_QK_HEADS * GDN_DK
GDN_VAL = GDN_V_HEADS * GDN_DV

VMEM_LIMIT = 56 * 1024 * 1024


def _params(*sem):
    return pltpu.CompilerParams(dimension_semantics=sem, vmem_limit_bytes=VMEM_LIMIT)


def _rms(x, gain):
    return x * lax.rsqrt(jnp.mean(x * x, axis=-1, keepdims=True) + RMS_EPS) * gain


def _softplus(z):
    return jnp.maximum(z, 0.0) + jnp.log1p(jnp.exp(-jnp.abs(z)))


def _silu(z):
    return z * jax.nn.sigmoid(z)


def _dot(a, b):
    return jnp.dot(a.astype(BF16), b.astype(BF16), preferred_element_type=F32)


def _dot_nt(a, b):
    return lax.dot_general(a.astype(BF16), b.astype(BF16), (((1,), (1,)), ((), ())),
                           preferred_element_type=F32)


def _dot_tn(a, b):
    return lax.dot_general(a.astype(BF16), b.astype(BF16), (((0,), (0,)), ((), ())),
                           preferred_element_type=F32)


def _dot_f32(a, b):
    return jnp.dot(a, b, precision=HIGHEST, preferred_element_type=F32)


def _iota2(shape, dim):
    return lax.broadcasted_iota(jnp.int32, shape, dim)


def _ffn_kernel(x_ref, g_ref, wg_ref, wu_ref, wd_ref, fg_ref, o_ref, n_ref, *, final_norm):
    j = pl.program_id(1)

    @pl.when(j == 0)
    def _():
        x = x_ref[...]
        n_ref[...] = _rms(x, g_ref[...]).astype(BF16)
        o_ref[...] = x

    n = n_ref[...]
    gate = jnp.dot(n, wg_ref[...].astype(BF16), preferred_element_type=F32)
    up = jnp.dot(n, wu_ref[...].astype(BF16), preferred_element_type=F32)
    act = (0.5 * _silu(gate) * up).astype(BF16)
    o_ref[...] += jnp.dot(act, wd_ref[...].astype(BF16), preferred_element_type=F32)

    if final_norm:
        @pl.when(j == pl.num_programs(1) - 1)
        def _():
            o_ref[...] = _rms(o_ref[...], fg_ref[...])


def _ffn(h, gain, wg, wu, wd, layer, idx, final_gain=None, tm=1024, tf=256):
    s, d = h.shape
    dff = wg.shape[-1]
    fg = jnp.ones((1, d), F32) if final_gain is None else final_gain.reshape(1, d)
    return pl.pallas_call(
        functools.partial(_ffn_kernel, final_norm=final_gain is not None),
        grid=(s // tm, dff // tf),
        in_specs=[
            pl.BlockSpec((tm, d), lambda i, j: (i, 0)),
            pl.BlockSpec((1, d), lambda i, j: (0, 0)),
            pl.BlockSpec((None, None, d, tf), lambda i, j: (layer, idx, 0, j)),
            pl.BlockSpec((None, None, d, tf), lambda i, j: (layer, idx, 0, j)),
            pl.BlockSpec((None, None, tf, d), lambda i, j: (layer, idx, j, 0)),
            pl.BlockSpec((1, d), lambda i, j: (0, 0)),
        ],
        out_specs=pl.BlockSpec((tm, d), lambda i, j: (i, 0)),
        out_shape=jax.ShapeDtypeStruct((s, d), F32),
        scratch_shapes=[pltpu.VMEM((tm, d), BF16)],
        compiler_params=_params("parallel", "arbitrary"),
        name="ffn",
    )(h, gain.reshape(1, d), wg, wu, wd, fg)


def _norm_proj_kernel(x_ref, g_ref, w_ref, we_ref, o_ref, e_ref, n_ref):
    j = pl.program_id(1)

    @pl.when(j == 0)
    def _():
        nb = _rms(x_ref[...], g_ref[...]).astype(BF16)
        n_ref[...] = nb
        e_ref[...] = jnp.dot(nb, we_ref[...], preferred_element_type=F32)

    acc = jnp.dot(n_ref[...], w_ref[...], preferred_element_type=F32)
    for c in range(o_ref.shape[0]):
        o_ref[c] = acc[:, c * LANE:(c + 1) * LANE].astype(o_ref.dtype)


def _norm_proj(h, gain, w_main, w_extra, n=None, tm=1024, tn=1024):
    s, d = h.shape
    n = w_main.shape[1] if n is None else n
    return pl.pallas_call(
        _norm_proj_kernel,
        grid=(s // tm, n // tn),
        in_specs=[
            pl.BlockSpec((tm, d), lambda i, j: (i, 0)),
            pl.BlockSpec((1, d), lambda i, j: (0, 0)),
            pl.BlockSpec((d, tn), lambda i, j: (0, j)),
            pl.BlockSpec((d, LANE), lambda i, j: (0, 0)),
        ],
        out_specs=[
            pl.BlockSpec((tn // LANE, tm, LANE), lambda i, j: (j, i, 0)),
            pl.BlockSpec((tm, LANE), lambda i, j: (i, 0)),
        ],
        out_shape=[
            jax.ShapeDtypeStruct((n // LANE, s, LANE), BF16),
            jax.ShapeDtypeStruct((s, LANE), F32),
        ],
        scratch_shapes=[pltpu.VMEM((tm, d), BF16)],
        compiler_params=_params("parallel", "arbitrary"),
        name="norm_proj",
    )(h, gain.reshape(1, d), w_main, w_extra)


def _proj_res_kernel(*refs, n_in):
    h_ref = refs[0]
    x_refs = refs[1:1 + n_in]
    w_refs = refs[1 + n_in:1 + 2 * n_in]
    o_ref = refs[1 + 2 * n_in]
    acc = h_ref[...]
    for x_ref, w_ref in zip(x_refs, w_refs):
        acc = acc + jnp.dot(x_ref[...], w_ref[...], preferred_element_type=F32)
    o_ref[...] = acc


def _proj_res(h, xs, ws, tm=1024, tn=1024):
    s, d = h.shape
    n_in = len(xs)
    in_specs = [pl.BlockSpec((tm, tn), lambda i, j: (i, j))]
    in_specs += [pl.BlockSpec((tm, x.shape[1]), lambda i, j: (i, 0)) for x in xs]
    in_specs += [pl.BlockSpec((w.shape[0], tn), lambda i, j: (0, j)) for w in ws]
    return pl.pallas_call(
        functools.partial(_proj_res_kernel, n_in=n_in),
        grid=(s // tm, d // tn),
        in_specs=in_specs,
        out_specs=pl.BlockSpec((tm, tn), lambda i, j: (i, j)),
        out_shape=jax.ShapeDtypeStruct((s, d), F32),
        compiler_params=_params("parallel", "arbitrary"),
        name="proj_res",
    )(h, *xs, *ws)


def _gla_kernel(q_ref, k_ref, v_ref, og_ref, lr_ref, w2_ref, b_ref, gn_ref, o_ref, st_ref, *, tb):
    t = pl.program_id(0)

    @pl.when(t == 0)
    def _():
        st_ref[...] = jnp.zeros_like(st_ref)

    tril = (_iota2((CHUNK, CHUNK), 0) >= _iota2((CHUNK, CHUNK), 1))
    tril_f = tril.astype(F32)
    gn = gn_ref[...]
    vs = GLA_DV // LANE
    z = _dot(lr_ref[...], w2_ref[...]) + b_ref[...]
    log_a = -_softplus(-z) / GLA_GATE_NORMALIZER
    chunks = range(tb // CHUNK)
    rows = [slice(c * CHUNK, (c + 1) * CHUNK) for c in chunks]
    units = [(h, c) for c in chunks for h in range(GLA_HEADS)]
    cum_all = [_dot_f32(tril_f, log_a[rows[c], :]) for c in chunks]
    intra, qe, upd, dec = {}, {}, {}, {}
    for u in units:
        h, c = u
        cum = cum_all[c][:, h * GLA_DK:(h + 1) * GLA_DK]
        mid = cum[CHUNK // 2 - 1:CHUNK // 2, :]
        last = cum[CHUNK - 1:CHUNK, :]
        q = q_ref[h, rows[c], :].astype(F32) * (GLA_DK ** -0.5)
        k = k_ref[h, rows[c], :].astype(F32)
        v = jnp.concatenate([v_ref[vs * h + i, rows[c], :] for i in range(vs)], axis=1)
        scores = _dot_nt(q * jnp.exp(cum - mid), k * jnp.exp(mid - cum))
        intra[u] = _dot(jnp.where(tril, scores, 0.0), v)
        qe[u] = q * jnp.exp(cum)
        upd[u] = _dot_tn(v, k * jnp.exp(last - cum))
        dec[u] = jnp.exp(last)
    for u in units:
        h, c = u
        st = st_ref[h]
        out = intra[u] + _dot_nt(qe[u], st)
        st_ref[h] = dec[u] * st + upd[u]
        og = jnp.concatenate([og_ref[vs * h + i, rows[c], :] for i in range(vs)], axis=1).astype(F32)
        o_ref[rows[c], h * GLA_DV:(h + 1) * GLA_DV] = (_rms(out, gn) * _silu(og)).astype(o_ref.dtype)


def _gla(slabs, extra, w2, bias, gnorm, tb=512):
    s = slabs.shape[1]
    nk = GLA_KEY // LANE
    return pl.pallas_call(
        functools.partial(_gla_kernel, tb=tb),
        grid=(s // tb,),
        in_specs=[
            pl.BlockSpec((nk, tb, LANE), lambda t: (0, t, 0)),
            pl.BlockSpec((nk, tb, LANE), lambda t: (1, t, 0)),
            pl.BlockSpec((2 * nk, tb, LANE), lambda t: (1, t, 0)),
            pl.BlockSpec((2 * nk, tb, LANE), lambda t: (2, t, 0)),
            pl.BlockSpec((tb, LANE), lambda t: (t, 0)),
            pl.BlockSpec((LANE, GLA_KEY), lambda t: (0, 0)),
            pl.BlockSpec((1, GLA_KEY), lambda t: (0, 0)),
            pl.BlockSpec((1, GLA_DV), lambda t: (0, 0)),
        ],
        out_specs=pl.BlockSpec((tb, GLA_VAL), lambda t: (t, 0)),
        out_shape=jax.ShapeDtypeStruct((s, GLA_VAL), BF16),
        scratch_shapes=[pltpu.VMEM((GLA_HEADS, GLA_DV, GLA_DK), F32)],
        compiler_params=_params("arbitrary"),
        name="gla",
    )(slabs, slabs, slabs, slabs, extra, w2, bias, gnorm)


def _conv_rows(hist_ref, w, r0, n):
    acc = None
    for kk in range(CONV_WIDTH):
        off = CONV_PAD + r0 - (CONV_WIDTH - 1) + kk
        term = w[kk:kk + 1, :] * hist_ref[off:off + n, :]
        acc = term if acc is None else acc + term
    return acc


def _conv_carry(hist_ref, tb):
    hist_ref[0:CONV_PAD, :] = hist_ref[tb:tb + CONV_PAD, :]


def _causal_conv(hist_ref, x, w, tb):
    hist_ref[CONV_PAD:CONV_PAD + tb, :] = x
    acc = _conv_rows(hist_ref, w, 0, tb)
    _conv_carry(hist_ref, tb)
    return acc


def _lru_kernel(x_ref, y_ref, cw_ref, cb_ref, wa_ref, ba_ref, wx_ref, bx_ref, lam_ref,
                o_ref, hist_ref, h_ref, *, tb):
    t = pl.program_id(1)

    @pl.when(t == 0)
    def _():
        hist_ref[0:CONV_PAD, :] = jnp.zeros((CONV_PAD, LANE), F32)
        h_ref[...] = jnp.zeros_like(h_ref)

    xc = _causal_conv(hist_ref, x_ref[0].astype(F32), cw_ref[...], tb) + cb_ref[...]
    r = jax.nn.sigmoid(_dot(xc, wa_ref[0]) + ba_ref[...])
    i = jax.nn.sigmoid(_dot(xc, wx_ref[0]) + bx_ref[...])
    log_a = -LRU_C * r * _softplus(-lam_ref[...])
    a = jnp.exp(log_a)
    one_m_a2 = -jnp.tanh(log_a) * (a * a + 1.0)
    u = jnp.where(one_m_a2 > 0.0, one_m_a2 * lax.rsqrt(one_m_a2), 0.0) * (i * xc)

    row = _iota2((tb, LANE), 0)
    sh = 1
    while sh < tb:
        if sh < SUBLANE:
            keep = row >= sh
            a_prev = jnp.where(keep, pltpu.roll(a, sh, 0), 1.0)
            u_prev = jnp.where(keep, pltpu.roll(u, sh, 0), 0.0)
            u = u + a * u_prev
            a = a * a_prev
        else:
            u = jnp.concatenate([u[:sh], u[sh:] + a[sh:] * u[:tb - sh]], axis=0)
            a = jnp.concatenate([a[:sh], a[sh:] * a[:tb - sh]], axis=0)
        sh *= 2
    h = u + a * h_ref[0:1, :]
    h_ref[...] = jnp.broadcast_to(h[tb - 1:tb, :], h_ref.shape)
    o_ref[...] = (h * jax.nn.gelu(y_ref[0].astype(F32))).astype(o_ref.dtype)


def _lru(slabs, x_slab0, y_slab0, conv_w, conv_b, w_a, b_a, w_x, b_x, lam, tb=1024):
    s = slabs.shape[1]
    row = lambda n, t: (0, n)
    return pl.pallas_call(
        functools.partial(_lru_kernel, tb=tb),
        grid=(LRU_BLOCKS, s // tb),
        in_specs=[
            pl.BlockSpec((1, tb, LANE), lambda n, t: (x_slab0 + n, t, 0)),
            pl.BlockSpec((1, tb, LANE), lambda n, t: (y_slab0 + n, t, 0)),
            pl.BlockSpec((CONV_WIDTH, LANE), row),
            pl.BlockSpec((1, LANE), row),
            pl.BlockSpec((1, LANE, LANE), lambda n, t: (n, 0, 0)),
            pl.BlockSpec((1, LANE), row),
            pl.BlockSpec((1, LANE, LANE), lambda n, t: (n, 0, 0)),
            pl.BlockSpec((1, LANE), row),
            pl.BlockSpec((1, LANE), row),
        ],
        out_specs=pl.BlockSpec((tb, LANE), lambda n, t: (t, n)),
        out_shape=jax.ShapeDtypeStruct((s, LRU_WIDTH), BF16),
        scratch_shapes=[pltpu.VMEM((tb + CONV_PAD, LANE), F32), pltpu.VMEM((8, LANE), F32)],
        compiler_params=_params("parallel", "arbitrary"),
        name="rg_lru",
    )(slabs, slabs, conv_w, conv_b.reshape(1, -1), w_a, b_a.reshape(1, -1), w_x,
      b_x.reshape(1, -1), lam.reshape(1, -1))


def _gdn_gates_kernel(e_ref, alog_ref, dtb_ref, cols_ref, rows_ref, *, tb):
    ex = e_ref[...]
    is_beta = _iota2((CHUNK, LANE), 1) < GDN_V_HEADS
    beta = jax.nn.sigmoid(ex)
    g = -jnp.exp(alog_ref[...]) * _softplus(ex + dtb_ref[...])
    tril_f = (_iota2((CHUNK, CHUNK), 0) >= _iota2((CHUNK, CHUNK), 1)).astype(F32)
    for c in range(tb // CHUNK):
        rows = slice(c * CHUNK, (c + 1) * CHUNK)
        gcum = _dot_f32(tril_f, g[rows, :])
        cols_ref[rows, :] = jnp.where(is_beta, beta[rows, :], gcum)
        rows_ref[c] = gcum.T


def _gdn_gates(extra, a_log, dt_bias, tb=512):
    s = extra.shape[0]
    alog_row = jnp.zeros((1, LANE), F32).at[0, GDN_V_HEADS:2 * GDN_V_HEADS].set(a_log)
    dtb_row = jnp.zeros((1, LANE), F32).at[0, GDN_V_HEADS:2 * GDN_V_HEADS].set(dt_bias)
    return pl.pallas_call(
        functools.partial(_gdn_gates_kernel, tb=tb),
        grid=(s // tb,),
        in_specs=[
            pl.BlockSpec((tb, LANE), lambda t: (t, 0)),
            pl.BlockSpec((1, LANE), lambda t: (0, 0)),
            pl.BlockSpec((1, LANE), lambda t: (0, 0)),
        ],
        out_specs=[
            pl.BlockSpec((tb, LANE), lambda t: (t, 0)),
            pl.BlockSpec((tb // CHUNK, LANE, CHUNK), lambda t: (t, 0, 0)),
        ],
        out_shape=[
            jax.ShapeDtypeStruct((s, LANE), F32),
            jax.ShapeDtypeStruct((s // CHUNK, LANE, CHUNK), F32),
        ],
        compiler_params=_params("parallel"),
        name="gdn_gates",
    )(extra, alog_row, dtb_row)


def _gdn_kernel(q_ref, k_ref, v_ref, z_ref, cols_ref, rows_ref, cwq_ref, cwk_ref, cwv_ref,
                on_ref, o_ref, qh_ref, kh_ref, vh_ref, qt_ref, kt_ref, vt_ref, st_ref, *, tb):
    hg = pl.program_id(0)
    t = pl.program_id(1)
    heads = range(GDN_GROUP)

    hists = ((qh_ref, qt_ref, q_ref), (kh_ref, kt_ref, k_ref), (vh_ref, vt_ref, v_ref))

    @pl.when(t == 0)
    def _():
        for _, tail_ref, _ in hists:
            tail_ref[...] = jnp.zeros_like(tail_ref)
        st_ref[...] = jnp.zeros_like(st_ref)

    for hist_ref, tail_ref, x_ref in hists:
        hist_ref[0:CONV_PAD_BF16, :] = tail_ref[...]
        for i in range(x_ref.shape[0]):
            hist_ref[CONV_PAD_BF16:CONV_PAD_BF16 + tb, i * LANE:(i + 1) * LANE] = x_ref[i]
    cwq, cwk, cwv = cwq_ref[...], cwk_ref[...], cwv_ref[...]

    ext = CONV_PAD_BF16 + CHUNK
    tap_row = _iota2((CONV_WIDTH * CHUNK, ext), 0)
    tap_col = _iota2((CONV_WIDTH * CHUNK, ext), 1)
    shift = (tap_col == tap_row % CHUNK + tap_row // CHUNK
             + (CONV_PAD_BF16 - (CONV_WIDTH - 1))).astype(BF16)

    def conv_silu(hist_ref, w, c):
        taps = jnp.dot(shift, hist_ref[c * CHUNK:c * CHUNK + ext, :], preferred_element_type=F32)
        acc = w[0:1, :] * taps[0:CHUNK]
        for kk in range(1, CONV_WIDTH):
            acc = acc + w[kk:kk + 1, :] * taps[kk * CHUNK:(kk + 1) * CHUNK]
        return _silu(acc)

    def l2n(x):
        return x * lax.rsqrt(jnp.sum(x * x, axis=-1, keepdims=True) + L2_EPS)

    lane = _iota2((CHUNK, LANE), 1)
    sub = _iota2((CHUNK, LANE), 0)
    left = lane < CHUNK
    col = jnp.where(left, lane, lane - CHUNK)
    incl = sub >= col
    strict = sub > col
    eye_p = (sub == col).astype(F32)
    on = on_ref[...]

    def block_diag(m):
        return jnp.concatenate([jnp.where(left, m, 0.0), jnp.where(left, 0.0, m)], axis=0)

    chunks = range(tb // CHUNK)
    units = [(j, c) for c in chunks for j in heads]
    vheads = range(GDN_REP)
    rows = [slice(c * CHUNK, (c + 1) * CHUNK) for c in chunks]
    q, k, v, beta, gcum, glast, decay, kq, xpw, tinv, attn = ({} for _ in range(11))
    qc = {c: conv_silu(qh_ref, cwq, c) for c in chunks}
    kc = {c: conv_silu(kh_ref, cwk, c) for c in chunks}
    v = {c: conv_silu(vh_ref, cwv, c) for c in chunks}
    for u in units:
        j, c = u
        q[u] = l2n(qc[c][:, j * LANE:(j + 1) * LANE]) * (GDN_DK ** -0.5)
        k[u] = l2n(kc[c][:, j * LANE:(j + 1) * LANE])
    for u in units:
        j, c = u
        cols = cols_ref[rows[c], :]
        hv = [(hg * GDN_GROUP + j) * GDN_REP + e for e in vheads]
        beta[u] = [jnp.sum(jnp.where(lane == h, cols, 0.0), axis=1, keepdims=True) for h in hv]
        gcum[u] = [jnp.sum(jnp.where(lane == h + GDN_V_HEADS, cols, 0.0), axis=1, keepdims=True)
                   for h in hv]
        grow = [rows_ref[c, pl.ds(GDN_V_HEADS + h, 1), :] for h in hv]
        grow_p = jnp.concatenate(grow, axis=1)
        glast[u] = [r[:, CHUNK - 1:CHUNK] for r in grow]
        gcol_p = jnp.where(left, gcum[u][0], gcum[u][1])
        decay[u] = jnp.where(incl, jnp.exp(jnp.where(incl, gcol_p - grow_p, 0.0)), 0.0)
    for u in units:
        kq[u] = _dot_nt(jnp.concatenate([k[u], q[u]], axis=0), jnp.concatenate([k[u], k[u]], axis=0))
    for u in units:
        beta_p = jnp.where(left, beta[u][0], beta[u][1])
        lmat = jnp.where(strict, kq[u][:CHUNK] * beta_p * decay[u], 0.0)
        attn[u] = kq[u][CHUNK:] * decay[u]
        xpw[u] = -lmat
        tinv[u] = eye_p + xpw[u]
    for u in units:
        xpw[u] = _dot(xpw[u], block_diag(xpw[u]))
    for _ in range(4):
        for u in units:
            both = _dot(jnp.concatenate([tinv[u], xpw[u]], axis=0), block_diag(xpw[u]))
            tinv[u] = tinv[u] + both[:CHUNK]
            xpw[u] = both[CHUNK:]
    for u in units:
        tinv[u] = block_diag(tinv[u] + _dot(tinv[u], block_diag(xpw[u])))
    sol, att_sol = {}, {}
    for u in units:
        j, c = u
        rhs = jnp.concatenate(
            [jnp.concatenate([v[c][:, (j * GDN_REP + e) * LANE:(j * GDN_REP + e + 1) * LANE]
                              * beta[u][e],
                              k[u] * (beta[u][e] * jnp.exp(gcum[u][e]))], axis=1)
             for e in vheads], axis=0)
        sol[u] = _dot(tinv[u], rhs)
    for u in units:
        att_sol[u] = _dot(block_diag(attn[u]), sol[u])
    lhs, upd = {}, {}
    for u in units:
        for e in vheads:
            hrows = slice(e * CHUNK, (e + 1) * CHUNK)
            kd = k[u] * jnp.exp(glast[u][e] - gcum[u][e])
            kd_sol = _dot_tn(kd, sol[u][hrows, :])
            qd = q[u] * jnp.exp(gcum[u][e]) - att_sol[u][hrows, GDN_DV:]
            lhs[u, e] = jnp.concatenate([kd_sol[:, GDN_DV:], qd], axis=0)
            upd[u, e] = kd_sol[:, :GDN_DV]
    for u in units:
        j, c = u
        for e in vheads:
            hrows = slice(e * CHUNK, (e + 1) * CHUNK)
            hv = j * GDN_REP + e
            st = st_ref[hv]
            prod = _dot(lhs[u, e], st)
            st_ref[hv] = jnp.exp(glast[u][e]) * st + upd[u, e] - prod[:GDN_DK]
            out = prod[GDN_DK:] + att_sol[u][hrows, :GDN_DV]
            zg = z_ref[hv, rows[c], :].astype(F32)
            o_ref[rows[c], hv * LANE:(hv + 1) * LANE] = (_rms(out, on) * _silu(zg)).astype(o_ref.dtype)
    for _, tail_ref, x_ref in hists:
        for i in range(x_ref.shape[0]):
            tail_ref[:, i * LANE:(i + 1) * LANE] = x_ref[i, tb - CONV_PAD_BF16:tb, :]


def _gdn(slabs, cols, rows, conv_w, out_norm, tb=512):
    s = slabs.shape[1]
    g = GDN_GROUP
    gv = GDN_GROUP * GDN_REP
    nq = GDN_QK_HEADS // g
    return pl.pallas_call(
        functools.partial(_gdn_kernel, tb=tb),
        grid=(nq, s // tb),
        in_specs=[
            pl.BlockSpec((g, tb, LANE), lambda h, t: (h, t, 0)),
            pl.BlockSpec((g, tb, LANE), lambda h, t: (nq + h, t, 0)),
            pl.BlockSpec((gv, tb, LANE), lambda h, t: (nq + h, t, 0)),
            pl.BlockSpec((gv, tb, LANE), lambda h, t: (2 * nq + h, t, 0)),
            pl.BlockSpec((tb, LANE), lambda h, t: (t, 0)),
            pl.BlockSpec((tb // CHUNK, LANE, CHUNK), lambda h, t: (t, 0, 0)),
            pl.BlockSpec((CONV_WIDTH, g * LANE), lambda h, t: (0, h)),
            pl.BlockSpec((CONV_WIDTH, g * LANE), lambda h, t: (0, nq + h)),
            pl.BlockSpec((CONV_WIDTH, gv * LANE), lambda h, t: (0, nq + h)),
            pl.BlockSpec((1, GDN_DV), lambda h, t: (0, 0)),
        ],
        out_specs=pl.BlockSpec((tb, gv * LANE), lambda h, t: (t, h)),
        out_shape=jax.ShapeDtypeStruct((s, GDN_VAL), BF16),
        scratch_shapes=[
            pltpu.VMEM((tb + CONV_PAD_BF16, g * LANE), BF16),
            pltpu.VMEM((tb + CONV_PAD_BF16, g * LANE), BF16),
            pltpu.VMEM((tb + CONV_PAD_BF16, gv * LANE), BF16),
            pltpu.VMEM((CONV_PAD_BF16, g * LANE), BF16),
            pltpu.VMEM((CONV_PAD_BF16, g * LANE), BF16),
            pltpu.VMEM((CONV_PAD_BF16, gv * LANE), BF16),
            pltpu.VMEM((gv, GDN_DK, GDN_DV), F32),
        ],
        compiler_params=_params("parallel", "arbitrary"),
        name="gdn",
    )(slabs, slabs, slabs, slabs, cols, rows, conv_w, conv_w, conv_w, out_norm.reshape(1, -1))


def _pad_cols(w, n):
    return jnp.pad(w, ((0, 0), (0, n - w.shape[1])))


def _mixer_a(h, gain, w_in, gate_w2, gate_b, gla_norm, conv_w, conv_b, w_a, b_a, w_x, b_x, lam, w_out):
    c_lr = 2 * GLA_KEY + 2 * GLA_VAL
    c_x = c_lr + GLA_GATE_RANK
    w_main = jnp.concatenate([w_in[:, :c_lr], w_in[:, c_x:]], axis=1).astype(BF16)
    w_extra = _pad_cols(w_in[:, c_lr:c_x], LANE).astype(BF16)
    slabs, extra = _norm_proj(h, gain, w_main, w_extra)
    w2 = jnp.pad(gate_w2, ((0, LANE - GLA_GATE_RANK), (0, 0)))
    o_gla = _gla(slabs, extra, w2, gate_b.reshape(1, -1), gla_norm.reshape(1, -1))
    x_slab0 = c_lr // LANE
    o_lru = _lru(slabs, x_slab0, x_slab0 + LRU_WIDTH // LANE, conv_w, conv_b, w_a, b_a, w_x, b_x, lam)
    w_out = w_out.astype(BF16)
    return _proj_res(h, [o_gla, o_lru], [w_out[:GLA_VAL], w_out[GLA_VAL:]])


def _mixer_c(h, gain, w_in, conv_w, a_log, dt_bias, out_norm, w_out):
    c_main = 2 * GDN_KEY + 2 * GDN_VAL
    w_extra = _pad_cols(w_in[:, c_main:], LANE).astype(BF16)
    slabs, extra = _norm_proj(h, gain, w_in.astype(BF16), w_extra, n=c_main, tn=2048)
    cols, rows = _gdn_gates(extra, a_log, dt_bias)
    o = _gdn(slabs, cols, rows, conv_w, out_norm)
    return _proj_res(h, [o], [w_out.astype(BF16)])


def kernel(x, norms, ffn_w_gate, ffn_w_up, ffn_w_down, a_w_in, a_gla_gate_w2, a_gla_gate_b, a_gla_norm, a_rg_conv_w, a_rg_conv_b, a_rg_w_a, a_rg_b_a, a_rg_w_x, a_rg_b_x, a_rg_lambda, a_w_out, c_w_in, c_conv_w, c_a_log, c_dt_bias, c_out_norm, c_w_out, final_norm):
    b, s, d = x.shape
    depth = norms.shape[0]
    wg, wu, wd = ffn_w_gate, ffn_w_up, ffn_w_down
    outs = []
    for bi in range(b):
        h = x.reshape(s, d) if b == 1 else x[bi]
        for layer in range(depth):
            j = layer // 2
            h = _ffn(h, norms[layer, 0], wg, wu, wd, layer, 0)
            if layer % 2 == 0:
                h = _mixer_a(h, norms[layer, 1], a_w_in[j], a_gla_gate_w2[j], a_gla_gate_b[j],
                             a_gla_norm[j], a_rg_conv_w[j], a_rg_conv_b[j], a_rg_w_a[j], a_rg_b_a[j],
                             a_rg_w_x[j], a_rg_b_x[j], a_rg_lambda[j], a_w_out[j])
            else:
                h = _mixer_c(h, norms[layer, 1], c_w_in[j], c_conv_w[j], c_a_log[j], c_dt_bias[j],
                             c_out_norm[j], c_w_out[j])
            last = layer == depth - 1
            h = _ffn(h, norms[layer, 2], wg, wu, wd, layer, 1,
                     final_gain=final_norm if last else None)
        outs.append(h)
    return outs[0].reshape(b, s, d) if b == 1 else jnp.stack(outs, axis=0)
```

```python
import functools

import jax
import jax.numpy as jnp
from jax import lax
from jax.experimental import pallas as pl
from jax.experimental.pallas import tpu as pltpu

F32 = jnp.float32
BF16 = jnp.bfloat16
HIGHEST = lax.Precision.HIGHEST

LANE = 128
D_MODEL = 2048
D_FF = 5632
RMS_EPS = 1e-6
L2_EPS = 1e-6
CONV_WIDTH = 4
SUBLANE = 8
CONV_PAD = SUBLANE
CONV_PAD_BF16 = 16

GLA_HEADS = 4
GLA_DK = 128
GLA_DV = 256
GLA_KEY = GLA_HEADS * GLA_DK
GLA_VAL = GLA_HEADS * GLA_DV
GLA_GATE_RANK = 16
GLA_GATE_NORMALIZER = 16.0
CHUNK = 64

LRU_WIDTH = 1024
LRU_BLOCKS = 8
LRU_C = 8.0

GDN_QK_HEADS = 16
GDN_V_HEADS = 32
GDN_REP = GDN_V_HEADS // GDN_QK_HEADS
GDN_GROUP = 4
GDN_DK = 128
GDN_DV = 128
GDN_KEY = GDN_QK_HEADS * GDN_DK
GDN_VAL = GDN_V_HEADS * GDN_DV

VMEM_LIMIT = 56 * 1024 * 1024


def _params(*sem):
    return pltpu.CompilerParams(dimension_semantics=sem, vmem_limit_bytes=VMEM_LIMIT)


def _rms(x, gain):
    return x * lax.rsqrt(jnp.mean(x * x, axis=-1, keepdims=True) + RMS_EPS) * gain


def _softplus(z):
    return jnp.maximum(z, 0.0) + jnp.log1p(jnp.exp(-jnp.abs(z)))


def _silu(z):
    return z * jax.nn.sigmoid(z)


def _dot(a, b):
    return jnp.dot(a.astype(BF16), b.astype(BF16), preferred_element_type=F32)


def _dot_nt(a, b):
    return lax.dot_general(a.astype(BF16), b.astype(BF16), (((1,), (1,)), ((), ())),
                           preferred_element_type=F32)


def _dot_tn(a, b):
    return lax.dot_general(a.astype(BF16), b.astype(BF16), (((0,), (0,)), ((), ())),
                           preferred_element_type=F32)


def _dot_f32(a, b):
    return jnp.dot(a, b, precision=HIGHEST, preferred_element_type=F32)


def _iota2(shape, dim):
    return lax.broadcasted_iota(jnp.int32, shape, dim)


def _ffn_kernel(x_ref, g_ref, wg_ref, wu_ref, wd_ref, fg_ref, o_ref, n_ref, *, final_norm):
    j = pl.program_id(1)

    @pl.when(j == 0)
    def _():
        x = x_ref[...]
        n_ref[...] = _rms(x, g_ref[...]).astype(BF16)
        o_ref[...] = x

    n = n_ref[...]
    gate = jnp.dot(n, wg_ref[...].astype(BF16), preferred_element_type=F32)
    up = jnp.dot(n, wu_ref[...].astype(BF16), preferred_element_type=F32)
    act = (0.5 * _silu(gate) * up).astype(BF16)
    o_ref[...] += jnp.dot(act, wd_ref[...].astype(BF16), preferred_element_type=F32)

    if final_norm:
        @pl.when(j == pl.num_programs(1) - 1)
        def _():
            o_ref[...] = _rms(o_ref[...], fg_ref[...])


def _ffn(h, gain, wg, wu, wd, layer, idx, final_gain=None, tm=1024, tf=256):
    s, d = h.shape
    dff = wg.shape[-1]
    fg = jnp.ones((1, d), F32) if final_gain is None else final_gain.reshape(1, d)
    return pl.pallas_call(
        functools.partial(_ffn_kernel, final_norm=final_gain is not None),
        grid=(s // tm, dff // tf),
        in_specs=[
            pl.BlockSpec((tm, d), lambda i, j: (i, 0)),
            pl.BlockSpec((1, d), lambda i, j: (0, 0)),
            pl.BlockSpec((None, None, d, tf), lambda i, j: (layer, idx, 0, j)),
            pl.BlockSpec((None, None, d, tf), lambda i, j: (layer, idx, 0, j)),
            pl.BlockSpec((None, None, tf, d), lambda i, j: (layer, idx, j, 0)),
            pl.BlockSpec((1, d), lambda i, j: (0, 0)),
        ],
        out_specs=pl.BlockSpec((tm, d), lambda i, j: (i, 0)),
        out_shape=jax.ShapeDtypeStruct((s, d), F32),
        scratch_shapes=[pltpu.VMEM((tm, d), BF16)],
        compiler_params=_params("parallel", "arbitrary"),
        name="ffn",
    )(h, gain.reshape(1, d), wg, wu, wd, fg)


def _norm_proj_kernel(x_ref, g_ref, w_ref, we_ref, o_ref, e_ref, n_ref):
    j = pl.program_id(1)

    @pl.when(j == 0)
    def _():
        nb = _rms(x_ref[...], g_ref[...]).astype(BF16)
        n_ref[...] = nb
        e_ref[...] = jnp.dot(nb, we_ref[...], preferred_element_type=F32)

    acc = jnp.dot(n_ref[...], w_ref[...], preferred_element_type=F32)
    for c in range(o_ref.shape[0]):
        o_ref[c] = acc[:, c * LANE:(c + 1) * LANE].astype(o_ref.dtype)


def _norm_proj(h, gain, w_main, w_extra, n=None, tm=1024, tn=1024):
    s, d = h.shape
    n = w_main.shape[1] if n is None else n
    return pl.pallas_call(
        _norm_proj_kernel,
        grid=(s // tm, n // tn),
        in_specs=[
            pl.BlockSpec((tm, d), lambda i, j: (i, 0)),
            pl.BlockSpec((1, d), lambda i, j: (0, 0)),
            pl.BlockSpec((d, tn), lambda i, j: (0, j)),
            pl.BlockSpec((d, LANE), lambda i, j: (0, 0)),
        ],
        out_specs=[
            pl.BlockSpec((tn // LANE, tm, LANE), lambda i, j: (j, i, 0)),
            pl.BlockSpec((tm, LANE), lambda i, j: (i, 0)),
        ],
        out_shape=[
            jax.ShapeDtypeStruct((n // LANE, s, LANE), BF16),
            jax.ShapeDtypeStruct((s, LANE), F32),
        ],
        scratch_shapes=[pltpu.VMEM((tm, d), BF16)],
        compiler_params=_params("parallel", "arbitrary"),
        name="norm_proj",
    )(h, gain.reshape(1, d), w_main, w_extra)


def _proj_res_kernel(*refs, n_in):
    h_ref = refs[0]
    x_refs = refs[1:1 + n_in]
    w_refs = refs[1 + n_in:1 + 2 * n_in]
    o_ref = refs[1 + 2 * n_in]
    acc = h_ref[...]
    for x_ref, w_ref in zip(x_refs, w_refs):
        acc = acc + jnp.dot(x_ref[...], w_ref[...], preferred_element_type=F32)
    o_ref[...] = acc


def _proj_res(h, xs, ws, tm=1024, tn=1024):
    s, d = h.shape
    n_in = len(xs)
    in_specs = [pl.BlockSpec((tm, tn), lambda i, j: (i, j))]
    in_specs += [pl.BlockSpec((tm, x.shape[1]), lambda i, j: (i, 0)) for x in xs]
    in_specs += [pl.BlockSpec((w.shape[0], tn), lambda i, j: (0, j)) for w in ws]
    return pl.pallas_call(
        functools.partial(_proj_res_kernel, n_in=n_in),
        grid=(s // tm, d // tn),
        in_specs=in_specs,
        out_specs=pl.BlockSpec((tm, tn), lambda i, j: (i, j)),
        out_shape=jax.ShapeDtypeStruct((s, d), F32),
        compiler_params=_params("parallel", "arbitrary"),
        name="proj_res",
    )(h, *xs, *ws)


def _gla_kernel(q_ref, k_ref, v_ref, og_ref, lr_ref, w2_ref, b_ref, gn_ref, o_ref, st_ref, *, tb):
    t = pl.program_id(0)

    @pl.when(t == 0)
    def _():
        st_ref[...] = jnp.zeros_like(st_ref)

    tril = (_iota2((CHUNK, CHUNK), 0) >= _iota2((CHUNK, CHUNK), 1))
    tril_f = tril.astype(F32)
    gn = gn_ref[...]
    vs = GLA_DV // LANE
    z = _dot(lr_ref[...], w2_ref[...]) + b_ref[...]
    log_a = -_softplus(-z) / GLA_GATE_NORMALIZER
    chunks = range(tb // CHUNK)
    rows = [slice(c * CHUNK, (c + 1) * CHUNK) for c in chunks]
    units = [(h, c) for c in chunks for h in range(GLA_HEADS)]
    cum_all = [_dot_f32(tril_f, log_a[rows[c], :]) for c in chunks]
    intra, qe, upd, dec = {}, {}, {}, {}
    for u in units:
        h, c = u
        cum = cum_all[c][:, h * GLA_DK:(h + 1) * GLA_DK]
        mid = cum[CHUNK // 2 - 1:CHUNK // 2, :]
        last = cum[CHUNK - 1:CHUNK, :]
        q = q_ref[h, rows[c], :].astype(F32) * (GLA_DK ** -0.5)
        k = k_ref[h, rows[c], :].astype(F32)
        v = jnp.concatenate([v_ref[vs * h + i, rows[c], :] for i in range(vs)], axis=1)
        scores = _dot_nt(q * jnp.exp(cum - mid), k * jnp.exp(mid - cum))
        intra[u] = _dot(jnp.where(tril, scores, 0.0), v)
        qe[u] = q * jnp.exp(cum)
        upd[u] = _dot_tn(v, k * jnp.exp(last - cum))
        dec[u] = jnp.exp(last)
    for u in units:
        h, c = u
        st = st_ref[h]
        out = intra[u] + _dot_nt(qe[u], st)
        st_ref[h] = dec[u] * st + upd[u]
        og = jnp.concatenate([og_ref[vs * h + i, rows[c], :] for i in range(vs)], axis=1).astype(F32)
        o_ref[rows[c], h * GLA_DV:(h + 1) * GLA_DV] = (_rms(out, gn) * _silu(og)).astype(o_ref.dtype)


def _gla(slabs, extra, w2, bias, gnorm, tb=512):
    s = slabs.shape[1]
    nk = GLA_KEY // LANE
    return pl.pallas_call(
        functools.partial(_gla_kernel, tb=tb),
        grid=(s // tb,),
        in_specs=[
            pl.BlockSpec((nk, tb, LANE), lambda t: (0, t, 0)),
            pl.BlockSpec((nk, tb, LANE), lambda t: (1, t, 0)),
            pl.BlockSpec((2 * nk, tb, LANE), lambda t: (1, t, 0)),
            pl.BlockSpec((2 * nk, tb, LANE), lambda t: (2, t, 0)),
            pl.BlockSpec((tb, LANE), lambda t: (t, 0)),
            pl.BlockSpec((LANE, GLA_KEY), lambda t: (0, 0)),
            pl.BlockSpec((1, GLA_KEY), lambda t: (0, 0)),
            pl.BlockSpec((1, GLA_DV), lambda t: (0, 0)),
        ],
        out_specs=pl.BlockSpec((tb, GLA_VAL), lambda t: (t, 0)),
        out_shape=jax.ShapeDtypeStruct((s, GLA_VAL), BF16),
        scratch_shapes=[pltpu.VMEM((GLA_HEADS, GLA_DV, GLA_DK), F32)],
        compiler_params=_params("arbitrary"),
        name="gla",
    )(slabs, slabs, slabs, slabs, extra, w2, bias, gnorm)


def _conv_rows(hist_ref, w, r0, n):
    acc = None
    for kk in range(CONV_WIDTH):
        off = CONV_PAD + r0 - (CONV_WIDTH - 1) + kk
        term = w[kk:kk + 1, :] * hist_ref[off:off + n, :]
        acc = term if acc is None else acc + term
    return acc


def _conv_carry(hist_ref, tb):
    hist_ref[0:CONV_PAD, :] = hist_ref[tb:tb + CONV_PAD, :]


def _causal_conv(hist_ref, x, w, tb):
    hist_ref[CONV_PAD:CONV_PAD + tb, :] = x
    acc = _conv_rows(hist_ref, w, 0, tb)
    _conv_carry(hist_ref, tb)
    return acc


def _lru_kernel(x_ref, y_ref, cw_ref, cb_ref, wa_ref, ba_ref, wx_ref, bx_ref, lam_ref,
                o_ref, hist_ref, h_ref, *, tb):
    t = pl.program_id(1)

    @pl.when(t == 0)
    def _():
        hist_ref[0:CONV_PAD, :] = jnp.zeros((CONV_PAD, LANE), F32)
        h_ref[...] = jnp.zeros_like(h_ref)

    xc = _causal_conv(hist_ref, x_ref[0].astype(F32), cw_ref[...], tb) + cb_ref[...]
    r = jax.nn.sigmoid(_dot(xc, wa_ref[0]) + ba_ref[...])
    i = jax.nn.sigmoid(_dot(xc, wx_ref[0]) + bx_ref[...])
    log_a = -LRU_C * r * _softplus(-lam_ref[...])
    a = jnp.exp(log_a)
    one_m_a2 = -jnp.tanh(log_a) * (a * a + 1.0)
    u = jnp.where(one_m_a2 > 0.0, one_m_a2 * lax.rsqrt(one_m_a2), 0.0) * (i * xc)

    row = _iota2((tb, LANE), 0)
    sh = 1
    while sh < tb:
        if sh < SUBLANE:
            keep = row >= sh
            a_prev = jnp.where(keep, pltpu.roll(a, sh, 0), 1.0)
            u_prev = jnp.where(keep, pltpu.roll(u, sh, 0), 0.0)
            u = u + a * u_prev
            a = a * a_prev
        else:
            u = jnp.concatenate([u[:sh], u[sh:] + a[sh:] * u[:tb - sh]], axis=0)
            a = jnp.concatenate([a[:sh], a[sh:] * a[:tb - sh]], axis=0)
        sh *= 2
    h = u + a * h_ref[0:1, :]
    h_ref[...] = jnp.broadcast_to(h[tb - 1:tb, :], h_ref.shape)
    o_ref[...] = (h * jax.nn.gelu(y_ref[0].astype(F32))).astype(o_ref.dtype)


def _lru(slabs, x_slab0, y_slab0, conv_w, conv_b, w_a, b_a, w_x, b_x, lam, tb=1024):
    s = slabs.shape[1]
    row = lambda n, t: (0, n)
    return pl.pallas_call(
        functools.partial(_lru_kernel, tb=tb),
        grid=(LRU_BLOCKS, s // tb),
        in_specs=[
            pl.BlockSpec((1, tb, LANE), lambda n, t: (x_slab0 + n, t, 0)),
            pl.BlockSpec((1, tb, LANE), lambda n, t: (y_slab0 + n, t, 0)),
            pl.BlockSpec((CONV_WIDTH, LANE), row),
            pl.BlockSpec((1, LANE), row),
            pl.BlockSpec((1, LANE, LANE), lambda n, t: (n, 0, 0)),
            pl.BlockSpec((1, LANE), row),
            pl.BlockSpec((1, LANE, LANE), lambda n, t: (n, 0, 0)),
            pl.BlockSpec((1, LANE), row),
            pl.BlockSpec((1, LANE), row),
        ],
        out_specs=pl.BlockSpec((tb, LANE), lambda n, t: (t, n)),
        out_shape=jax.ShapeDtypeStruct((s, LRU_WIDTH), BF16),
        scratch_shapes=[pltpu.VMEM((tb + CONV_PAD, LANE), F32), pltpu.VMEM((8, LANE), F32)],
        compiler_params=_params("parallel", "arbitrary"),
        name="rg_lru",
    )(slabs, slabs, conv_w, conv_b.reshape(1, -1), w_a, b_a.reshape(1, -1), w_x,
      b_x.reshape(1, -1), lam.reshape(1, -1))


def _gdn_gates_kernel(e_ref, alog_ref, dtb_ref, cols_ref, rows_ref, *, tb):
    ex = e_ref[...]
    is_beta = _iota2((CHUNK, LANE), 1) < GDN_V_HEADS
    beta = jax.nn.sigmoid(ex)
    g = -jnp.exp(alog_ref[...]) * _softplus(ex + dtb_ref[...])
    tril_f = (_iota2((CHUNK, CHUNK), 0) >= _iota2((CHUNK, CHUNK), 1)).astype(F32)
    for c in range(tb // CHUNK):
        rows = slice(c * CHUNK, (c + 1) * CHUNK)
        gcum = _dot_f32(tril_f, g[rows, :])
        cols_ref[rows, :] = jnp.where(is_beta, beta[rows, :], gcum)
        rows_ref[c] = gcum.T


def _gdn_gates(extra, a_log, dt_bias, tb=512):
    s = extra.shape[0]
    alog_row = jnp.zeros((1, LANE), F32).at[0, GDN_V_HEADS:2 * GDN_V_HEADS].set(a_log)
    dtb_row = jnp.zeros((1, LANE), F32).at[0, GDN_V_HEADS:2 * GDN_V_HEADS].set(dt_bias)
    return pl.pallas_call(
        functools.partial(_gdn_gates_kernel, tb=tb),
        grid=(s // tb,),
        in_specs=[
            pl.BlockSpec((tb, LANE), lambda t: (t, 0)),
            pl.BlockSpec((1, LANE), lambda t: (0, 0)),
            pl.BlockSpec((1, LANE), lambda t: (0, 0)),
        ],
        out_specs=[
            pl.BlockSpec((tb, LANE), lambda t: (t, 0)),
            pl.BlockSpec((tb // CHUNK, LANE, CHUNK), lambda t: (t, 0, 0)),
        ],
        out_shape=[
            jax.ShapeDtypeStruct((s, LANE), F32),
            jax.ShapeDtypeStruct((s // CHUNK, LANE, CHUNK), F32),
        ],
        compiler_params=_params("parallel"),
        name="gdn_gates",
    )(extra, alog_row, dtb_row)


def _gdn_kernel(q_ref, k_ref, v_ref, z_ref, cols_ref, rows_ref, cwq_ref, cwk_ref, cwv_ref,
                on_ref, o_ref, qh_ref, kh_ref, vh_ref, qt_ref, kt_ref, vt_ref, st_ref, *, tb):
    hg = pl.program_id(0)
    t = pl.program_id(1)
    heads = range(GDN_GROUP)

    hists = ((qh_ref, qt_ref, q_ref), (kh_ref, kt_ref, k_ref), (vh_ref, vt_ref, v_ref))

    @pl.when(t == 0)
    def _():
        for _, tail_ref, _ in hists:
            tail_ref[...] = jnp.zeros_like(tail_ref)
        st_ref[...] = jnp.zeros_like(st_ref)

    for hist_ref, tail_ref, x_ref in hists:
        hist_ref[0:CONV_PAD_BF16, :] = tail_ref[...]
        for i in range(x_ref.shape[0]):
            hist_ref[CONV_PAD_BF16:CONV_PAD_BF16 + tb, i * LANE:(i + 1) * LANE] = x_ref[i]
    cwq, cwk, cwv = cwq_ref[...], cwk_ref[...], cwv_ref[...]

    ext = CONV_PAD_BF16 + CHUNK
    n_shift = CONV_WIDTH - 1
    tap_row = _iota2((n_shift * CHUNK, ext), 0)
    tap_col = _iota2((n_shift * CHUNK, ext), 1)
    shift = (tap_col == tap_row % CHUNK + tap_row // CHUNK + (CONV_PAD_BF16 - n_shift)).astype(BF16)

    def conv_silu(hist_ref, w, c):
        lo = c * CHUNK
        taps = jnp.dot(shift, hist_ref[lo:lo + ext, :], preferred_element_type=F32)
        acc = w[n_shift:CONV_WIDTH, :] * hist_ref[lo + CONV_PAD_BF16:lo + ext, :].astype(F32)
        for kk in range(n_shift):
            acc = acc + w[kk:kk + 1, :] * taps[kk * CHUNK:(kk + 1) * CHUNK]
        return _silu(acc)

    def l2n(x):
        return x * lax.rsqrt(jnp.sum(x * x, axis=-1, keepdims=True) + L2_EPS)

    lane = _iota2((CHUNK, LANE), 1)
    sub = _iota2((CHUNK, LANE), 0)
    left = lane < CHUNK
    col = jnp.where(left, lane, lane - CHUNK)
    incl = sub >= col
    strict = sub > col
    eye_p = (sub == col).astype(F32)
    on = on_ref[...]

    def block_diag(m):
        return jnp.concatenate([jnp.where(left, m, 0.0), jnp.where(left, 0.0, m)], axis=0)

    chunks = range(tb // CHUNK)
    units = [(j, c) for c in chunks for j in heads]
    vheads = range(GDN_REP)
    rows = [slice(c * CHUNK, (c + 1) * CHUNK) for c in chunks]
    q, k, v, beta, gcum, glast, decay, kq, xpw, tinv, attn = ({} for _ in range(11))
    qc = {c: conv_silu(qh_ref, cwq, c) for c in chunks}
    kc = {c: conv_silu(kh_ref, cwk, c) for c in chunks}
    v = {c: conv_silu(vh_ref, cwv, c) for c in chunks}
    for u in units:
        j, c = u
        q[u] = l2n(qc[c][:, j * LANE:(j + 1) * LANE]) * (GDN_DK ** -0.5)
        k[u] = l2n(kc[c][:, j * LANE:(j + 1) * LANE])
    for u in units:
        j, c = u
        cols = cols_ref[rows[c], :]
        hv = [(hg * GDN_GROUP + j) * GDN_REP + e for e in vheads]
        beta[u] = [jnp.sum(jnp.where(lane == h, cols, 0.0), axis=1, keepdims=True) for h in hv]
        gcum[u] = [jnp.sum(jnp.where(lane == h + GDN_V_HEADS, cols, 0.0), axis=1, keepdims=True)
                   for h in hv]
        grow = [rows_ref[c, pl.ds(GDN_V_HEADS + h, 1), :] for h in hv]
        grow_p = jnp.concatenate(grow, axis=1)
        glast[u] = [r[:, CHUNK - 1:CHUNK] for r in grow]
        gcol_p = jnp.where(left, gcum[u][0], gcum[u][1])
        decay[u] = jnp.where(incl, jnp.exp(jnp.where(incl, gcol_p - grow_p, 0.0)), 0.0)
    for u in units:
        kq[u] = _dot_nt(jnp.concatenate([k[u], q[u]], axis=0), jnp.concatenate([k[u], k[u]], axis=0))
    for u in units:
        beta_p = jnp.where(left, beta[u][0], beta[u][1])
        lmat = jnp.where(strict, kq[u][:CHUNK] * beta_p * decay[u], 0.0)
        attn[u] = kq[u][CHUNK:] * decay[u]
        xpw[u] = -lmat
        tinv[u] = eye_p + xpw[u]
    for u in units:
        xpw[u] = _dot(xpw[u], block_diag(xpw[u]))
    for _ in range(4):
        for u in units:
            both = _dot(jnp.concatenate([tinv[u], xpw[u]], axis=0), block_diag(xpw[u]))
            tinv[u] = tinv[u] + both[:CHUNK]
            xpw[u] = both[CHUNK:]
    for u in units:
        tinv[u] = block_diag(tinv[u] + _dot(tinv[u], block_diag(xpw[u])))
    sol, att_sol = {}, {}
    for u in units:
        j, c = u
        rhs = jnp.concatenate(
            [jnp.concatenate([v[c][:, (j * GDN_REP + e) * LANE:(j * GDN_REP + e + 1) * LANE]
                              * beta[u][e],
                              k[u] * (beta[u][e] * jnp.exp(gcum[u][e]))], axis=1)
             for e in vheads], axis=0)
        sol[u] = _dot(tinv[u], rhs)
    for u in units:
        att_sol[u] = _dot(block_diag(attn[u]), sol[u])
    lhs, upd = {}, {}
    for u in units:
        for e in vheads:
            hrows = slice(e * CHUNK, (e + 1) * CHUNK)
            kd = k[u] * jnp.exp(glast[u][e] - gcum[u][e])
            kd_sol = _dot_tn(kd, sol[u][hrows, :])
            qd = q[u] * jnp.exp(gcum[u][e]) - att_sol[u][hrows, GDN_DV:]
            lhs[u, e] = jnp.concatenate([kd_sol[:, GDN_DV:], qd], axis=0)
            upd[u, e] = kd_sol[:, :GDN_DV]
    for u in units:
        j, c = u
        for e in vheads:
            hrows = slice(e * CHUNK, (e + 1) * CHUNK)
            hv = j * GDN_REP + e
            st = st_ref[hv]
            prod = _dot(lhs[u, e], st)
            st_ref[hv] = jnp.exp(glast[u][e]) * st + upd[u, e] - prod[:GDN_DK]
            out = prod[GDN_DK:] + att_sol[u][hrows, :GDN_DV]
            zg = z_ref[hv, rows[c], :].astype(F32)
            o_ref[rows[c], hv * LANE:(hv + 1) * LANE] = (_rms(out, on) * _silu(zg)).astype(o_ref.dtype)
    for _, tail_ref, x_ref in hists:
        for i in range(x_ref.shape[0]):
            tail_ref[:, i * LANE:(i + 1) * LANE] = x_ref[i, tb - CONV_PAD_BF16:tb, :]


def _gdn(slabs, cols, rows, conv_w, out_norm, tb=512):
    s = slabs.shape[1]
    g = GDN_GROUP
    gv = GDN_GROUP * GDN_REP
    nq = GDN_QK_HEADS // g
    return pl.pallas_call(
        functools.partial(_gdn_kernel, tb=tb),
        grid=(nq, s // tb),
        in_specs=[
            pl.BlockSpec((g, tb, LANE), lambda h, t: (h, t, 0)),
            pl.BlockSpec((g, tb, LANE), lambda h, t: (nq + h, t, 0)),
            pl.BlockSpec((gv, tb, LANE), lambda h, t: (nq + h, t, 0)),
            pl.BlockSpec((gv, tb, LANE), lambda h, t: (2 * nq + h, t, 0)),
            pl.BlockSpec((tb, LANE), lambda h, t: (t, 0)),
            pl.BlockSpec((tb // CHUNK, LANE, CHUNK), lambda h, t: (t, 0, 0)),
            pl.BlockSpec((CONV_WIDTH, g * LANE), lambda h, t: (0, h)),
            pl.BlockSpec((CONV_WIDTH, g * LANE), lambda h, t: (0, nq + h)),
            pl.BlockSpec((CONV_WIDTH, gv * LANE), lambda h, t: (0, nq + h)),
            pl.BlockSpec((1, GDN_DV), lambda h, t: (0, 0)),
        ],
        out_specs=pl.BlockSpec((tb, gv * LANE), lambda h, t: (t, h)),
        out_shape=jax.ShapeDtypeStruct((s, GDN_VAL), BF16),
        scratch_shapes=[
            pltpu.VMEM((tb + CONV_PAD_BF16, g * LANE), BF16),
            pltpu.VMEM((tb + CONV_PAD_BF16, g * LANE), BF16),
            pltpu.VMEM((tb + CONV_PAD_BF16, gv * LANE), BF16),
            pltpu.VMEM((CONV_PAD_BF16, g * LANE), BF16),
            pltpu.VMEM((CONV_PAD_BF16, g * LANE), BF16),
            pltpu.VMEM((CONV_PAD_BF16, gv * LANE), BF16),
            pltpu.VMEM((gv, GDN_DK, GDN_DV), F32),
        ],
        compiler_params=_params("parallel", "arbitrary"),
        name="gdn",
    )(slabs, slabs, slabs, slabs, cols, rows, conv_w, conv_w, conv_w, out_norm.reshape(1, -1))


def _pad_cols(w, n):
    return jnp.pad(w, ((0, 0), (0, n - w.shape[1])))


def _mixer_a(h, gain, w_in, gate_w2, gate_b, gla_norm, conv_w, conv_b, w_a, b_a, w_x, b_x, lam, w_out):
    c_lr = 2 * GLA_KEY + 2 * GLA_VAL
    c_x = c_lr + GLA_GATE_RANK
    w_main = jnp.concatenate([w_in[:, :c_lr], w_in[:, c_x:]], axis=1).astype(BF16)
    w_extra = _pad_cols(w_in[:, c_lr:c_x], LANE).astype(BF16)
    slabs, extra = _norm_proj(h, gain, w_main, w_extra)
    w2 = jnp.pad(gate_w2, ((0, LANE - GLA_GATE_RANK), (0, 0)))
    o_gla = _gla(slabs, extra, w2, gate_b.reshape(1, -1), gla_norm.reshape(1, -1))
    x_slab0 = c_lr // LANE
    o_lru = _lru(slabs, x_slab0, x_slab0 + LRU_WIDTH // LANE, conv_w, conv_b, w_a, b_a, w_x, b_x, lam)
    w_out = w_out.astype(BF16)
    return _proj_res(h, [o_gla, o_lru], [w_out[:GLA_VAL], w_out[GLA_VAL:]])


def _mixer_c(h, gain, w_in, conv_w, a_log, dt_bias, out_norm, w_out):
    c_main = 2 * GDN_KEY + 2 * GDN_VAL
    w_extra = _pad_cols(w_in[:, c_main:], LANE).astype(BF16)
    slabs, extra = _norm_proj(h, gain, w_in.astype(BF16), w_extra, n=c_main, tn=2048)
    cols, rows = _gdn_gates(extra, a_log, dt_bias)
    o = _gdn(slabs, cols, rows, conv_w, out_norm)
    return _proj_res(h, [o], [w_out.astype(BF16)])


def kernel(x, norms, ffn_w_gate, ffn_w_up, ffn_w_down, a_w_in, a_gla_gate_w2, a_gla_gate_b, a_gla_norm, a_rg_conv_w, a_rg_conv_b, a_rg_w_a, a_rg_b_a, a_rg_w_x, a_rg_b_x, a_rg_lambda, a_w_out, c_w_in, c_conv_w, c_a_log, c_dt_bias, c_out_norm, c_w_out, final_norm):
    b, s, d = x.shape
    depth = norms.shape[0]
    wg, wu, wd = ffn_w_gate, ffn_w_up, ffn_w_down
    outs = []
    for bi in range(b):
        h = x.reshape(s, d) if b == 1 else x[bi]
        for layer in range(depth):
            j = layer // 2
            h = _ffn(h, norms[layer, 0], wg, wu, wd, layer, 0)
            if layer % 2 == 0:
                h = _mixer_a(h, norms[layer, 1], a_w_in[j], a_gla_gate_w2[j], a_gla_gate_b[j],
                             a_gla_norm[j], a_rg_conv_w[j], a_rg_conv_b[j], a_rg_w_a[j], a_rg_b_a[j],
                             a_rg_w_x[j], a_rg_b_x[j], a_rg_lambda[j], a_w_out[j])
            else:
                h = _mixer_c(h, norms[layer, 1], c_w_in[j], c_conv_w[j], c_a_log[j], c_dt_bias[j],
                             c_out_norm[j], c_w_out[j])
            last = layer == depth - 1
            h = _ffn(h, norms[layer, 2], wg, wu, wd, layer, 1,
                     final_gain=final_norm if last else None)
        outs.append(h)
    return outs[0].reshape(b, s, d) if b == 1 else jnp.stack(outs, axis=0)
```

```python
import functools

import jax
import jax.numpy as jnp
from jax import lax
from jax.experimental import pallas as pl
from jax.experimental.pallas import tpu as pltpu

F32 = jnp.float32
BF16 = jnp.bfloat16
HIGHEST = lax.Precision.HIGHEST

LANE = 128
D_MODEL = 2048
D_FF = 5632
RMS_EPS = 1e-6
L2_EPS = 1e-6
CONV_WIDTH = 4
SUBLANE = 8
CONV_PAD = SUBLANE
CONV_PAD_BF16 = 16

GLA_HEADS = 4
GLA_DK = 128
GLA_DV = 256
GLA_KEY = GLA_HEADS * GLA_DK
GLA_VAL = GLA_HEADS * GLA_DV
GLA_GATE_RANK = 16
GLA_GATE_NORMALIZER = 16.0
CHUNK = 64

LRU_WIDTH = 1024
LRU_BLOCKS = 8
LRU_C = 8.0

GDN_QK_HEADS = 16
GDN_V_HEADS = 32
GDN_REP = GDN_V_HEADS // GDN_QK_HEADS
GDN_GROUP = 4
GDN_DK = 128
GDN_DV = 128
GDN_KEY = GDN_QK_HEADS * GDN_DK
GDN_VAL = GDN_V_HEADS * GDN_DV

VMEM_LIMIT = 56 * 1024 * 1024


def _params(*sem):
    return pltpu.CompilerParams(dimension_semantics=sem, vmem_limit_bytes=VMEM_LIMIT)


def _rms(x, gain):
    return x * lax.rsqrt(jnp.mean(x * x, axis=-1, keepdims=True) + RMS_EPS) * gain


def _softplus(z):
    return jnp.maximum(z, 0.0) + jnp.log1p(jnp.exp(-jnp.abs(z)))


def _silu(z):
    return z * jax.nn.sigmoid(z)


def _dot(a, b):
    return jnp.dot(a.astype(BF16), b.astype(BF16), preferred_element_type=F32)


def _dot_nt(a, b):
    return lax.dot_general(a.astype(BF16), b.astype(BF16), (((1,), (1,)), ((), ())),
                           preferred_element_type=F32)


def _dot_tn(a, b):
    return lax.dot_general(a.astype(BF16), b.astype(BF16), (((0,), (0,)), ((), ())),
                           preferred_element_type=F32)


def _dot_f32(a, b):
    return jnp.dot(a, b, precision=HIGHEST, preferred_element_type=F32)


def _iota2(shape, dim):
    return lax.broadcasted_iota(jnp.int32, shape, dim)


def _ffn_kernel(x_ref, g_ref, wg_ref, wu_ref, wd_ref, fg_ref, o_ref, n_ref, *, final_norm):
    j = pl.program_id(1)

    @pl.when(j == 0)
    def _():
        x = x_ref[...]
        n_ref[...] = _rms(x, g_ref[...]).astype(BF16)
        o_ref[...] = x

    n = n_ref[...]
    gate = jnp.dot(n, wg_ref[...].astype(BF16), preferred_element_type=F32)
    up = jnp.dot(n, wu_ref[...].astype(BF16), preferred_element_type=F32)
    act = (0.5 * _silu(gate) * up).astype(BF16)
    o_ref[...] += jnp.dot(act, wd_ref[...].astype(BF16), preferred_element_type=F32)

    if final_norm:
        @pl.when(j == pl.num_programs(1) - 1)
        def _():
            o_ref[...] = _rms(o_ref[...], fg_ref[...])


def _ffn(h, gain, wg, wu, wd, layer, idx, final_gain=None, tm=1024, tf=256):
    s, d = h.shape
    dff = wg.shape[-1]
    fg = jnp.ones((1, d), F32) if final_gain is None else final_gain.reshape(1, d)
    return pl.pallas_call(
        functools.partial(_ffn_kernel, final_norm=final_gain is not None),
        grid=(s // tm, dff // tf),
        in_specs=[
            pl.BlockSpec((tm, d), lambda i, j: (i, 0)),
            pl.BlockSpec((1, d), lambda i, j: (0, 0)),
            pl.BlockSpec((None, None, d, tf), lambda i, j: (layer, idx, 0, j)),
            pl.BlockSpec((None, None, d, tf), lambda i, j: (layer, idx, 0, j)),
            pl.BlockSpec((None, None, tf, d), lambda i, j: (layer, idx, j, 0)),
            pl.BlockSpec((1, d), lambda i, j: (0, 0)),
        ],
        out_specs=pl.BlockSpec((tm, d), lambda i, j: (i, 0)),
        out_shape=jax.ShapeDtypeStruct((s, d), F32),
        scratch_shapes=[pltpu.VMEM((tm, d), BF16)],
        compiler_params=_params("parallel", "arbitrary"),
        name="ffn",
    )(h, gain.reshape(1, d), wg, wu, wd, fg)


def _norm_proj_kernel(x_ref, g_ref, w_ref, we_ref, o_ref, e_ref, n_ref):
    j = pl.program_id(1)

    @pl.when(j == 0)
    def _():
        nb = _rms(x_ref[...], g_ref[...]).astype(BF16)
        n_ref[...] = nb
        e_ref[...] = jnp.dot(nb, we_ref[...], preferred_element_type=F32)

    acc = jnp.dot(n_ref[...], w_ref[...], preferred_element_type=F32)
    for c in range(o_ref.shape[0]):
        o_ref[c] = acc[:, c * LANE:(c + 1) * LANE].astype(o_ref.dtype)


def _norm_proj(h, gain, w_main, w_extra, n=None, tm=1024, tn=1024):
    s, d = h.shape
    n = w_main.shape[1] if n is None else n
    return pl.pallas_call(
        _norm_proj_kernel,
        grid=(s // tm, n // tn),
        in_specs=[
            pl.BlockSpec((tm, d), lambda i, j: (i, 0)),
            pl.BlockSpec((1, d), lambda i, j: (0, 0)),
            pl.BlockSpec((d, tn), lambda i, j: (0, j)),
            pl.BlockSpec((d, LANE), lambda i, j: (0, 0)),
        ],
        out_specs=[
            pl.BlockSpec((tn // LANE, tm, LANE), lambda i, j: (j, i, 0)),
            pl.BlockSpec((tm, LANE), lambda i, j: (i, 0)),
        ],
        out_shape=[
            jax.ShapeDtypeStruct((n // LANE, s, LANE), BF16),
            jax.ShapeDtypeStruct((s, LANE), F32),
        ],
        scratch_shapes=[pltpu.VMEM((tm, d), BF16)],
        compiler_params=_params("parallel", "arbitrary"),
        name="norm_proj",
    )(h, gain.reshape(1, d), w_main, w_extra)


def _proj_res_kernel(*refs, n_in):
    h_ref = refs[0]
    x_refs = refs[1:1 + n_in]
    w_refs = refs[1 + n_in:1 + 2 * n_in]
    o_ref = refs[1 + 2 * n_in]
    acc = h_ref[...]
    for x_ref, w_ref in zip(x_refs, w_refs):
        acc = acc + jnp.dot(x_ref[...], w_ref[...], preferred_element_type=F32)
    o_ref[...] = acc


def _proj_res(h, xs, ws, tm=1024, tn=1024):
    s, d = h.shape
    n_in = len(xs)
    in_specs = [pl.BlockSpec((tm, tn), lambda i, j: (i, j))]
    in_specs += [pl.BlockSpec((tm, x.shape[1]), lambda i, j: (i, 0)) for x in xs]
    in_specs += [pl.BlockSpec((w.shape[0], tn), lambda i, j: (0, j)) for w in ws]
    return pl.pallas_call(
        functools.partial(_proj_res_kernel, n_in=n_in),
        grid=(s // tm, d // tn),
        in_specs=in_specs,
        out_specs=pl.BlockSpec((tm, tn), lambda i, j: (i, j)),
        out_shape=jax.ShapeDtypeStruct((s, d), F32),
        compiler_params=_params("parallel", "arbitrary"),
        name="proj_res",
    )(h, *xs, *ws)


def _gla_kernel(q_ref, k_ref, v_ref, og_ref, lr_ref, w2_ref, b_ref, gn_ref, o_ref, st_ref, *, tb):
    t = pl.program_id(0)

    @pl.when(t == 0)
    def _():
        st_ref[...] = jnp.zeros_like(st_ref)

    tril = (_iota2((CHUNK, CHUNK), 0) >= _iota2((CHUNK, CHUNK), 1))
    tril_f = tril.astype(F32)
    gn = gn_ref[...]
    vs = GLA_DV // LANE
    z = _dot(lr_ref[...], w2_ref[...]) + b_ref[...]
    log_a = -_softplus(-z) / GLA_GATE_NORMALIZER
    chunks = range(tb // CHUNK)
    rows = [slice(c * CHUNK, (c + 1) * CHUNK) for c in chunks]
    units = [(h, c) for c in chunks for h in range(GLA_HEADS)]
    cum_all = [_dot_f32(tril_f, log_a[rows[c], :]) for c in chunks]
    intra, qe, upd, dec = {}, {}, {}, {}
    for u in units:
        h, c = u
        cum = cum_all[c][:, h * GLA_DK:(h + 1) * GLA_DK]
        mid = cum[CHUNK // 2 - 1:CHUNK // 2, :]
        last = cum[CHUNK - 1:CHUNK, :]
        q = q_ref[h, rows[c], :].astype(F32) * (GLA_DK ** -0.5)
        k = k_ref[h, rows[c], :].astype(F32)
        v = jnp.concatenate([v_ref[vs * h + i, rows[c], :] for i in range(vs)], axis=1)
        scores = _dot_nt(q * jnp.exp(cum - mid), k * jnp.exp(mid - cum))
        intra[u] = _dot(jnp.where(tril, scores, 0.0), v)
        qe[u] = q * jnp.exp(cum)
        upd[u] = _dot_tn(v, k * jnp.exp(last - cum))
        dec[u] = jnp.exp(last)
    for u in units:
        h, c = u
        st = st_ref[h]
        out = intra[u] + _dot_nt(qe[u], st)
        st_ref[h] = dec[u] * st + upd[u]
        og = jnp.concatenate([og_ref[vs * h + i, rows[c], :] for i in range(vs)], axis=1).astype(F32)
        o_ref[rows[c], h * GLA_DV:(h + 1) * GLA_DV] = (_rms(out, gn) * _silu(og)).astype(o_ref.dtype)


def _gla(slabs, extra, w2, bias, gnorm, tb=512):
    s = slabs.shape[1]
    nk = GLA_KEY // LANE
    return pl.pallas_call(
        functools.partial(_gla_kernel, tb=tb),
        grid=(s // tb,),
        in_specs=[
            pl.BlockSpec((nk, tb, LANE), lambda t: (0, t, 0)),
            pl.BlockSpec((nk, tb, LANE), lambda t: (1, t, 0)),
            pl.BlockSpec((2 * nk, tb, LANE), lambda t: (1, t, 0)),
            pl.BlockSpec((2 * nk, tb, LANE), lambda t: (2, t, 0)),
            pl.BlockSpec((tb, LANE), lambda t: (t, 0)),
            pl.BlockSpec((LANE, GLA_KEY), lambda t: (0, 0)),
            pl.BlockSpec((1, GLA_KEY), lambda t: (0, 0)),
            pl.BlockSpec((1, GLA_DV), lambda t: (0, 0)),
        ],
        out_specs=pl.BlockSpec((tb, GLA_VAL), lambda t: (t, 0)),
        out_shape=jax.ShapeDtypeStruct((s, GLA_VAL), BF16),
        scratch_shapes=[pltpu.VMEM((GLA_HEADS, GLA_DV, GLA_DK), F32)],
        compiler_params=_params("arbitrary"),
        name="gla",
    )(slabs, slabs, slabs, slabs, extra, w2, bias, gnorm)


def _conv_rows(hist_ref, w, r0, n):
    acc = None
    for kk in range(CONV_WIDTH):
        off = CONV_PAD + r0 - (CONV_WIDTH - 1) + kk
        term = w[kk:kk + 1, :] * hist_ref[off:off + n, :]
        acc = term if acc is None else acc + term
    return acc


def _conv_carry(hist_ref, tb):
    hist_ref[0:CONV_PAD, :] = hist_ref[tb:tb + CONV_PAD, :]


def _causal_conv(hist_ref, x, w, tb):
    hist_ref[CONV_PAD:CONV_PAD + tb, :] = x
    acc = _conv_rows(hist_ref, w, 0, tb)
    _conv_carry(hist_ref, tb)
    return acc


def _lru_kernel(x_ref, y_ref, cw_ref, cb_ref, wa_ref, ba_ref, wx_ref, bx_ref, lam_ref,
                o_ref, hist_ref, h_ref, *, tb):
    t = pl.program_id(1)

    @pl.when(t == 0)
    def _():
        hist_ref[0:CONV_PAD, :] = jnp.zeros((CONV_PAD, LANE), F32)
        h_ref[...] = jnp.zeros_like(h_ref)

    xc = _causal_conv(hist_ref, x_ref[0].astype(F32), cw_ref[...], tb) + cb_ref[...]
    r = jax.nn.sigmoid(_dot(xc, wa_ref[0]) + ba_ref[...])
    i = jax.nn.sigmoid(_dot(xc, wx_ref[0]) + bx_ref[...])
    log_a = -LRU_C * r * _softplus(-lam_ref[...])
    a = jnp.exp(log_a)
    one_m_a2 = -jnp.tanh(log_a) * (a * a + 1.0)
    u = jnp.where(one_m_a2 > 0.0, one_m_a2 * lax.rsqrt(one_m_a2), 0.0) * (i * xc)

    row = _iota2((tb, LANE), 0)
    sh = 1
    while sh < tb:
        if sh < SUBLANE:
            keep = row >= sh
            a_prev = jnp.where(keep, pltpu.roll(a, sh, 0), 1.0)
            u_prev = jnp.where(keep, pltpu.roll(u, sh, 0), 0.0)
            u = u + a * u_prev
            a = a * a_prev
        else:
            u = jnp.concatenate([u[:sh], u[sh:] + a[sh:] * u[:tb - sh]], axis=0)
            a = jnp.concatenate([a[:sh], a[sh:] * a[:tb - sh]], axis=0)
        sh *= 2
    h = u + a * h_ref[0:1, :]
    h_ref[...] = jnp.broadcast_to(h[tb - 1:tb, :], h_ref.shape)
    o_ref[...] = (h * jax.nn.gelu(y_ref[0].astype(F32))).astype(o_ref.dtype)


def _lru(slabs, x_slab0, y_slab0, conv_w, conv_b, w_a, b_a, w_x, b_x, lam, tb=1024):
    s = slabs.shape[1]
    row = lambda n, t: (0, n)
    return pl.pallas_call(
        functools.partial(_lru_kernel, tb=tb),
        grid=(LRU_BLOCKS, s // tb),
        in_specs=[
            pl.BlockSpec((1, tb, LANE), lambda n, t: (x_slab0 + n, t, 0)),
            pl.BlockSpec((1, tb, LANE), lambda n, t: (y_slab0 + n, t, 0)),
            pl.BlockSpec((CONV_WIDTH, LANE), row),
            pl.BlockSpec((1, LANE), row),
            pl.BlockSpec((1, LANE, LANE), lambda n, t: (n, 0, 0)),
            pl.BlockSpec((1, LANE), row),
            pl.BlockSpec((1, LANE, LANE), lambda n, t: (n, 0, 0)),
            pl.BlockSpec((1, LANE), row),
            pl.BlockSpec((1, LANE), row),
        ],
        out_specs=pl.BlockSpec((tb, LANE), lambda n, t: (t, n)),
        out_shape=jax.ShapeDtypeStruct((s, LRU_WIDTH), BF16),
        scratch_shapes=[pltpu.VMEM((tb + CONV_PAD, LANE), F32), pltpu.VMEM((8, LANE), F32)],
        compiler_params=_params("parallel", "arbitrary"),
        name="rg_lru",
    )(slabs, slabs, conv_w, conv_b.reshape(1, -1), w_a, b_a.reshape(1, -1), w_x,
      b_x.reshape(1, -1), lam.reshape(1, -1))


def _gdn_gates_kernel(e_ref, alog_ref, dtb_ref, cols_ref, rows_ref, *, tb):
    ex = e_ref[...]
    is_beta = _iota2((CHUNK, LANE), 1) < GDN_V_HEADS
    beta = jax.nn.sigmoid(ex)
    g = -jnp.exp(alog_ref[...]) * _softplus(ex + dtb_ref[...])
    tril_f = (_iota2((CHUNK, CHUNK), 0) >= _iota2((CHUNK, CHUNK), 1)).astype(F32)
    for c in range(tb // CHUNK):
        rows = slice(c * CHUNK, (c + 1) * CHUNK)
        gcum = _dot_f32(tril_f, g[rows, :])
        cols_ref[rows, :] = jnp.where(is_beta, beta[rows, :], gcum)
        rows_ref[c] = gcum.T


def _gdn_gates(extra, a_log, dt_bias, tb=2048):
    s = extra.shape[0]
    alog_row = jnp.zeros((1, LANE), F32).at[0, GDN_V_HEADS:2 * GDN_V_HEADS].set(a_log)
    dtb_row = jnp.zeros((1, LANE), F32).at[0, GDN_V_HEADS:2 * GDN_V_HEADS].set(dt_bias)
    return pl.pallas_call(
        functools.partial(_gdn_gates_kernel, tb=tb),
        grid=(s // tb,),
        in_specs=[
            pl.BlockSpec((tb, LANE), lambda t: (t, 0)),
            pl.BlockSpec((1, LANE), lambda t: (0, 0)),
            pl.BlockSpec((1, LANE), lambda t: (0, 0)),
        ],
        out_specs=[
            pl.BlockSpec((tb, LANE), lambda t: (t, 0)),
            pl.BlockSpec((tb // CHUNK, LANE, CHUNK), lambda t: (t, 0, 0)),
        ],
        out_shape=[
            jax.ShapeDtypeStruct((s, LANE), F32),
            jax.ShapeDtypeStruct((s // CHUNK, LANE, CHUNK), F32),
        ],
        compiler_params=_params("parallel"),
        name="gdn_gates",
    )(extra, alog_row, dtb_row)


def _gdn_kernel(q_ref, k_ref, v_ref, z_ref, cols_ref, rows_ref, cwq_ref, cwk_ref, cwv_ref,
                on_ref, o_ref, qh_ref, kh_ref, vh_ref, qt_ref, kt_ref, vt_ref, st_ref, *, tb):
    hg = pl.program_id(0)
    t = pl.program_id(1)
    heads = range(GDN_GROUP)

    hists = ((qh_ref, qt_ref, q_ref), (kh_ref, kt_ref, k_ref), (vh_ref, vt_ref, v_ref))

    @pl.when(t == 0)
    def _():
        for _, tail_ref, _ in hists:
            tail_ref[...] = jnp.zeros_like(tail_ref)
        st_ref[...] = jnp.zeros_like(st_ref)

    for hist_ref, tail_ref, x_ref in hists:
        hist_ref[0:CONV_PAD_BF16, :] = tail_ref[...]
        for i in range(x_ref.shape[0]):
            hist_ref[CONV_PAD_BF16:CONV_PAD_BF16 + tb, i * LANE:(i + 1) * LANE] = x_ref[i]
    cwq, cwk, cwv = cwq_ref[...], cwk_ref[...], cwv_ref[...]

    ext = CONV_PAD_BF16 + CHUNK
    tap_row = _iota2((CONV_WIDTH * CHUNK, ext), 0)
    tap_col = _iota2((CONV_WIDTH * CHUNK, ext), 1)
    shift = (tap_col == tap_row % CHUNK + tap_row // CHUNK
             + (CONV_PAD_BF16 - (CONV_WIDTH - 1))).astype(BF16)

    def conv_silu(hist_ref, w, c):
        taps = jnp.dot(shift, hist_ref[c * CHUNK:c * CHUNK + ext, :], preferred_element_type=F32)
        acc = w[0:1, :] * taps[0:CHUNK]
        for kk in range(1, CONV_WIDTH):
            acc = acc + w[kk:kk + 1, :] * taps[kk * CHUNK:(kk + 1) * CHUNK]
        return _silu(acc)

    def l2n(x):
        return x * lax.rsqrt(jnp.sum(x * x, axis=-1, keepdims=True) + L2_EPS)

    lane = _iota2((CHUNK, LANE), 1)
    sub = _iota2((CHUNK, LANE), 0)
    left = lane < CHUNK
    col = jnp.where(left, lane, lane - CHUNK)
    incl = sub >= col
    strict = sub > col
    eye_p = (sub == col).astype(F32)
    on = on_ref[...]

    def block_diag(m):
        return jnp.concatenate([jnp.where(left, m, 0.0), jnp.where(left, 0.0, m)], axis=0)

    chunks = range(tb // CHUNK)
    units = [(j, c) for c in chunks for j in heads]
    vheads = range(GDN_REP)
    rows = [slice(c * CHUNK, (c + 1) * CHUNK) for c in chunks]
    q, k, v, beta, gcum, glast, decay, kq, xpw, tinv, attn = ({} for _ in range(11))
    qc = {c: conv_silu(qh_ref, cwq, c) for c in chunks}
    kc = {c: conv_silu(kh_ref, cwk, c) for c in chunks}
    v = {c: conv_silu(vh_ref, cwv, c) for c in chunks}
    for u in units:
        j, c = u
        q[u] = l2n(qc[c][:, j * LANE:(j + 1) * LANE]) * (GDN_DK ** -0.5)
        k[u] = l2n(kc[c][:, j * LANE:(j + 1) * LANE])
    for u in units:
        j, c = u
        cols = cols_ref[rows[c], :]
        hv = [(hg * GDN_GROUP + j) * GDN_REP + e for e in vheads]
        beta[u] = [jnp.sum(jnp.where(lane == h, cols, 0.0), axis=1, keepdims=True) for h in hv]
        gcum[u] = [jnp.sum(jnp.where(lane == h + GDN_V_HEADS, cols, 0.0), axis=1, keepdims=True)
                   for h in hv]
        grow = [rows_ref[c, pl.ds(GDN_V_HEADS + h, 1), :] for h in hv]
        grow_p = jnp.concatenate(grow, axis=1)
        glast[u] = [r[:, CHUNK - 1:CHUNK] for r in grow]
        gcol_p = jnp.where(left, gcum[u][0], gcum[u][1])
        decay[u] = jnp.where(incl, jnp.exp(jnp.where(incl, gcol_p - grow_p, 0.0)), 0.0)
    for u in units:
        kq[u] = _dot_nt(jnp.concatenate([k[u], q[u]], axis=0), jnp.concatenate([k[u], k[u]], axis=0))
    for u in units:
        beta_p = jnp.where(left, beta[u][0], beta[u][1])
        lmat = jnp.where(strict, kq[u][:CHUNK] * beta_p * decay[u], 0.0)
        attn[u] = kq[u][CHUNK:] * decay[u]
        xpw[u] = -lmat
        tinv[u] = eye_p + xpw[u]
    for u in units:
        xpw[u] = _dot(xpw[u], block_diag(xpw[u]))
    for _ in range(4):
        for u in units:
            both = _dot(jnp.concatenate([tinv[u], xpw[u]], axis=0), block_diag(xpw[u]))
            tinv[u] = tinv[u] + both[:CHUNK]
            xpw[u] = both[CHUNK:]
    for u in units:
        tinv[u] = block_diag(tinv[u] + _dot(tinv[u], block_diag(xpw[u])))
    sol, att_sol = {}, {}
    for u in units:
        j, c = u
        rhs = jnp.concatenate(
            [jnp.concatenate([v[c][:, (j * GDN_REP + e) * LANE:(j * GDN_REP + e + 1) * LANE]
                              * beta[u][e],
                              k[u] * (beta[u][e] * jnp.exp(gcum[u][e]))], axis=1)
             for e in vheads], axis=0)
        sol[u] = _dot(tinv[u], rhs)
    for u in units:
        att_sol[u] = _dot(block_diag(attn[u]), sol[u])
    lhs, upd = {}, {}
    for u in units:
        for e in vheads:
            hrows = slice(e * CHUNK, (e + 1) * CHUNK)
            kd = k[u] * jnp.exp(glast[u][e] - gcum[u][e])
            kd_sol = _dot_tn(kd, sol[u][hrows, :])
            qd = q[u] * jnp.exp(gcum[u][e]) - att_sol[u][hrows, GDN_DV:]
            lhs[u, e] = jnp.concatenate([kd_sol[:, GDN_DV:], qd], axis=0)
            upd[u, e] = kd_sol[:, :GDN_DV]
    for u in units:
        j, c = u
        for e in vheads:
            hrows = slice(e * CHUNK, (e + 1) * CHUNK)
            hv = j * GDN_REP + e
            st = st_ref[hv]
            prod = _dot(lhs[u, e], st)
            st_ref[hv] = jnp.exp(glast[u][e]) * st + upd[u, e] - prod[:GDN_DK]
            out = prod[GDN_DK:] + att_sol[u][hrows, :GDN_DV]
            zg = z_ref[hv, rows[c], :].astype(F32)
            o_ref[rows[c], hv * LANE:(hv + 1) * LANE] = (_rms(out, on) * _silu(zg)).astype(o_ref.dtype)
    for _, tail_ref, x_ref in hists:
        for i in range(x_ref.shape[0]):
            tail_ref[:, i * LANE:(i + 1) * LANE] = x_ref[i, tb - CONV_PAD_BF16:tb, :]


def _gdn(slabs, cols, rows, conv_w, out_norm, tb=512):
    s = slabs.shape[1]
    g = GDN_GROUP
    gv = GDN_GROUP * GDN_REP
    nq = GDN_QK_HEADS // g
    return pl.pallas_call(
        functools.partial(_gdn_kernel, tb=tb),
        grid=(nq, s // tb),
        in_specs=[
            pl.BlockSpec((g, tb, LANE), lambda h, t: (h, t, 0)),
            pl.BlockSpec((g, tb, LANE), lambda h, t: (nq + h, t, 0)),
            pl.BlockSpec((gv, tb, LANE), lambda h, t: (nq + h, t, 0)),
            pl.BlockSpec((gv, tb, LANE), lambda h, t: (2 * nq + h, t, 0)),
            pl.BlockSpec((tb, LANE), lambda h, t: (t, 0)),
            pl.BlockSpec((tb // CHUNK, LANE, CHUNK), lambda h, t: (t, 0, 0)),
            pl.BlockSpec((CONV_WIDTH, g * LANE), lambda h, t: (0, h)),
            pl.BlockSpec((CONV_WIDTH, g * LANE), lambda h, t: (0, nq + h)),
            pl.BlockSpec((CONV_WIDTH, gv * LANE), lambda h, t: (0, nq + h)),
            pl.BlockSpec((1, GDN_DV), lambda h, t: (0, 0)),
        ],
        out_specs=pl.BlockSpec((tb, gv * LANE), lambda h, t: (t, h)),
        out_shape=jax.ShapeDtypeStruct((s, GDN_VAL), BF16),
        scratch_shapes=[
            pltpu.VMEM((tb + CONV_PAD_BF16, g * LANE), BF16),
            pltpu.VMEM((tb + CONV_PAD_BF16, g * LANE), BF16),
            pltpu.VMEM((tb + CONV_PAD_BF16, gv * LANE), BF16),
            pltpu.VMEM((CONV_PAD_BF16, g * LANE), BF16),
            pltpu.VMEM((CONV_PAD_BF16, g * LANE), BF16),
            pltpu.VMEM((CONV_PAD_BF16, gv * LANE), BF16),
            pltpu.VMEM((gv, GDN_DK, GDN_DV), F32),
        ],
        compiler_params=_params("parallel", "arbitrary"),
        name="gdn",
    )(slabs, slabs, slabs, slabs, cols, rows, conv_w, conv_w, conv_w, out_norm.reshape(1, -1))


def _pad_cols(w, n):
    return jnp.pad(w, ((0, 0), (0, n - w.shape[1])))


def _mixer_a(h, gain, w_in, gate_w2, gate_b, gla_norm, conv_w, conv_b, w_a, b_a, w_x, b_x, lam, w_out):
    c_lr = 2 * GLA_KEY + 2 * GLA_VAL
    c_x = c_lr + GLA_GATE_RANK
    w_main = jnp.concatenate([w_in[:, :c_lr], w_in[:, c_x:]], axis=1).astype(BF16)
    w_extra = _pad_cols(w_in[:, c_lr:c_x], LANE).astype(BF16)
    slabs, extra = _norm_proj(h, gain, w_main, w_extra, tn=1280)
    w2 = jnp.pad(gate_w2, ((0, LANE - GLA_GATE_RANK), (0, 0)))
    o_gla = _gla(slabs, extra, w2, gate_b.reshape(1, -1), gla_norm.reshape(1, -1))
    x_slab0 = c_lr // LANE
    o_lru = _lru(slabs, x_slab0, x_slab0 + LRU_WIDTH // LANE, conv_w, conv_b, w_a, b_a, w_x, b_x, lam)
    w_out = w_out.astype(BF16)
    return _proj_res(h, [o_gla, o_lru], [w_out[:GLA_VAL], w_out[GLA_VAL:]])


def _mixer_c(h, gain, w_in, conv_w, a_log, dt_bias, out_norm, w_out):
    c_main = 2 * GDN_KEY + 2 * GDN_VAL
    w_extra = _pad_cols(w_in[:, c_main:], LANE).astype(BF16)
    slabs, extra = _norm_proj(h, gain, w_in.astype(BF16), w_extra, n=c_main, tn=2048)
    cols, rows = _gdn_gates(extra, a_log, dt_bias)
    o = _gdn(slabs, cols, rows, conv_w, out_norm)
    return _proj_res(h, [o], [w_out.astype(BF16)])


def kernel(x, norms, ffn_w_gate, ffn_w_up, ffn_w_down, a_w_in, a_gla_gate_w2, a_gla_gate_b, a_gla_norm, a_rg_conv_w, a_rg_conv_b, a_rg_w_a, a_rg_b_a, a_rg_w_x, a_rg_b_x, a_rg_lambda, a_w_out, c_w_in, c_conv_w, c_a_log, c_dt_bias, c_out_norm, c_w_out, final_norm):
    b, s, d = x.shape
    depth = norms.shape[0]
    wg, wu, wd = ffn_w_gate, ffn_w_up, ffn_w_down
    outs = []
    for bi in range(b):
        h = x.reshape(s, d) if b == 1 else x[bi]
        for layer in range(depth):
            j = layer // 2
            h = _ffn(h, norms[layer, 0], wg, wu, wd, layer, 0)
            if layer % 2 == 0:
                h = _mixer_a(h, norms[layer, 1], a_w_in[j], a_gla_gate_w2[j], a_gla_gate_b[j],
                             a_gla_norm[j], a_rg_conv_w[j], a_rg_conv_b[j], a_rg_w_a[j], a_rg_b_a[j],
                             a_rg_w_x[j], a_rg_b_x[j], a_rg_lambda[j], a_w_out[j])
            else:
                h = _mixer_c(h, norms[layer, 1], c_w_in[j], c_conv_w[j], c_a_log[j], c_dt_bias[j],
                             c_out_norm[j], c_w_out[j])
            last = layer == depth - 1
            h = _ffn(h, norms[layer, 2], wg, wu, wd, layer, 1,
                     final_gain=final_norm if last else None)
        outs.append(h)
    return outs[0].reshape(b, s, d) if b == 1 else jnp.stack(outs, axis=0)
```
